```python
import functools
import jax, jax.numpy as jnp
from jax import lax
import numpy as np

D_MODEL = 1024
BATCH = 4
SEQ = 4096
DEPTH = 1
DEC_BATCH = 128
DEC_SEQ = 4
PAST_LEN = 8192
PAGE_SIZE = 128

MIX_WIDTH = D_MODEL
HEAD_DIM = 64
RWKV_WIDTH = MIX_WIDTH // 2
RWKV_HEADS = RWKV_WIDTH // HEAD_DIM
ATTN_WIDTH = MIX_WIDTH - RWKV_WIDTH
ATTN_HEADS = ATTN_WIDTH // HEAD_DIM
KV_HEADS = 2
GROUP = ATTN_HEADS // KV_HEADS
DECAY_LORA = 64
AAA_LORA = 64
GATE_LORA = 128
N_RW_COLS = 3 * RWKV_WIDTH + DECAY_LORA + AAA_LORA + GATE_LORA
N_IN_COLS = N_RW_COLS + ATTN_WIDTH + 2 * KV_HEADS * HEAD_DIM
LNX_EPS = 64e-5
WINDOW = 128
ATT_BLOCK = 128
ATTN_SCALE = HEAD_DIM ** -0.5
ROPE_THETA = 500000.0
ROT_DIM = HEAD_DIM // 4
N_META = 16
N_KEYS = 128
N_EXPERTS = N_KEYS * N_KEYS
PEER_HEADS = 8
PEER_TOPK = 16
D_KEY = 256
D_HALF = D_KEY // 2
PEER_BLOCK = 256
NORM_EPS = 1e-5
NEG_INF = -1e30
F32 = jnp.float32

kernel_name = 'hymba_rwkv7_swa_sink_peer_step'


def rmsnorm(x, g):
    xf = x.astype(F32)
    y = xf * lax.rsqrt(jnp.mean(xf * xf, axis=-1, keepdims=True) + NORM_EPS)
    return (y * g.astype(F32)).astype(x.dtype)


def rope_partial(x, pos):
    half = ROT_DIM // 2
    inv_freq = ROPE_THETA ** (-jnp.arange(0, ROT_DIM, 2, dtype=F32) / ROT_DIM)
    ang = pos.astype(F32)[:, None] * inv_freq[None, :]
    cos = jnp.cos(ang)[None, :, None, :]
    sin = jnp.sin(ang)[None, :, None, :]
    xr = x[..., :ROT_DIM].astype(F32)
    x1, x2 = xr[..., :half], xr[..., half:]
    rot = jnp.concatenate([x1 * cos - x2 * sin, x2 * cos + x1 * sin], axis=-1).astype(x.dtype)
    return jnp.concatenate([rot, x[..., ROT_DIM:]], axis=-1)


def sink_probs(scores, mask, sinks):
    s = jnp.where(mask, scores, NEG_INF)
    sk = sinks.astype(F32)[:, :, None, None]
    m = jnp.maximum(jnp.max(s, axis=-1, keepdims=True), sk)
    p = jnp.exp(s - m)
    return p / (jnp.sum(p, axis=-1, keepdims=True) + jnp.exp(sk - m))


def attend_banded(q, k, v, sinks):
    B, L = q.shape[:2]
    front = (-L) % ATT_BLOCK
    n_blk = (L + front) // ATT_BLOCK
    padw = ((0, 0), (front, 0), (0, 0), (0, 0))
    qb = jnp.pad(q, padw).reshape(B, n_blk, ATT_BLOCK, KV_HEADS, GROUP, HEAD_DIM)
    kb = jnp.pad(k, padw).reshape(B, n_blk, ATT_BLOCK, KV_HEADS, HEAD_DIM)
    vb = jnp.pad(v, padw).reshape(B, n_blk, ATT_BLOCK, KV_HEADS, HEAD_DIM)
    prev = lambda t: jnp.pad(t, ((0, 0), (1, 0), (0, 0), (0, 0), (0, 0)))[:, :-1]
    keys = jnp.concatenate([prev(kb), kb], axis=2)
    vals = jnp.concatenate([prev(vb), vb], axis=2)
    qpos = (jnp.arange(L + front) - front).reshape(n_blk, ATT_BLOCK)
    kpos = jnp.concatenate([qpos - ATT_BLOCK, qpos], axis=1)
    dlt = qpos[:, :, None] - kpos[:, None, :]
    mask = (kpos[:, None, :] >= 0) & (dlt >= 0) & (dlt < WINDOW)
    scores = jnp.einsum('bnqkgd,bnskd->bnkgqs', qb, keys).astype(F32) * ATTN_SCALE
    p = sink_probs(scores, mask[None, :, None, None], sinks.reshape(KV_HEADS, GROUP))
    out = jnp.einsum('bnkgqs,bnskd->bnqkgd', p.astype(v.dtype), vals)
    out = out.reshape(B, L + front, ATTN_WIDTH)[:, front:]
    keep = min(WINDOW, L)
    return out, k[:, L - keep:], v[:, L - keep:]


def attend_window(q, k, v, sinks, k_buf, v_buf):
    B, S = q.shape[:2]
    wb = k_buf.shape[1]
    keys = jnp.concatenate([k_buf.astype(k.dtype), k], axis=1)
    vals = jnp.concatenate([v_buf.astype(v.dtype), v], axis=1)
    qpos = PAST_LEN + jnp.arange(S)
    kpos = jnp.concatenate([PAST_LEN - wb + jnp.arange(wb), qpos])
    dlt = qpos[:, None] - kpos[None, :]
    mask = (dlt >= 0) & (dlt < WINDOW)
    qg = q.reshape(B, S, KV_HEADS, GROUP, HEAD_DIM)
    scores = jnp.einsum('bqkgd,bskd->bkgqs', qg, keys).astype(F32) * ATTN_SCALE
    p = sink_probs(scores, mask, sinks.reshape(KV_HEADS, GROUP))
    out = jnp.einsum('bkgqs,bskd->bqkgd', p.astype(v.dtype), vals).reshape(B, S, ATTN_WIDTH)
    return out, keys[:, -wb:], vals[:, -wb:]


def wkv7_scan(r, decay, k, v, a_vec, b_vec, s0):
    def step(S, inp):
        r_t, w_t, k_t, v_t, a_t, b_t = inp
        sa = jnp.einsum('bhij,bhj->bhi', S, a_t)
        S = S * w_t[:, :, None, :] + sa[..., :, None] * b_t[:, :, None, :] + v_t[..., :, None] * k_t[:, :, None, :]
        y = jnp.einsum('bhij,bhj->bhi', S, r_t)
        return S, y
    xs = tuple(jnp.moveaxis(t, 1, 0) for t in (r, decay, k, v, a_vec, b_vec))
    S, ys = lax.scan(step, s0, xs)
    return jnp.moveaxis(ys, 0, 1), S


def rwkv7_time_mix(p, p_prev_row, s0, lp):
    B, T, _ = p.shape
    p_prev = jnp.concatenate([p_prev_row[:, None, :].astype(p.dtype), p[:, :-1]], axis=1)
    m = (p + (p_prev - p) * lp['mu_shift']).astype(F32)
    c = RWKV_WIDTH
    cuts = [c, 2 * c, 3 * c, 3 * c + DECAY_LORA, 3 * c + DECAY_LORA + AAA_LORA]
    xr, xk, xv, xw, xa, xg = jnp.split(m, cuts, axis=-1)
    w = -jax.nn.softplus(-(lp['w0'] + jnp.tanh(xw) @ lp['w_lora_w2'])) - 0.5
    decay = jnp.exp(-jnp.exp(w))
    a = jax.nn.sigmoid(lp['a0'] + xa @ lp['w_lora_a2'])
    g = jax.nn.sigmoid(xg) @ lp['w_lora_g2']
    heads = lambda t: t.reshape(B, T, RWKV_HEADS, HEAD_DIM)
    kk = heads(xk * lp['k_k'])
    kk = kk / jnp.maximum(jnp.sqrt(jnp.sum(kk * kk, axis=-1, keepdims=True)), 1e-12)
    kmod = xk * (1.0 + (a - 1.0) * lp['k_a'])
    r_h, k_h, v_h, a_h = heads(xr), heads(kmod), heads(xv), heads(a)
    y, S = wkv7_scan(r_h, heads(decay), k_h, v_h, -kk, kk * a_h, s0.astype(F32))
    mean = jnp.mean(y, axis=-1, keepdims=True)
    var = jnp.mean(jnp.square(y - mean), axis=-1, keepdims=True)
    y = ((y - mean) * lax.rsqrt(var + LNX_EPS)).reshape(B, T, RWKV_WIDTH) * lp['lnx_w'] + lp['lnx_b']
    y = y + (jnp.sum(r_h * k_h * lp['r_k'], axis=-1, keepdims=True) * v_h).reshape(B, T, RWKV_WIDTH)
    y = y * g
    return y.astype(p.dtype), S


def peer(h, lp):
    shp = h.shape
    t = h.reshape(-1, D_MODEL)
    n = t.shape[0]
    n_blk = -(-n // PEER_BLOCK)
    blocks = jnp.pad(t, ((0, n_blk * PEER_BLOCK - n), (0, 0))).reshape(n_blk, PEER_BLOCK, D_MODEL)
    w_query, sub_keys, eu, ev = lp['w_query'], lp['sub_keys'], lp['expert_u'], lp['expert_v']

    def block(hb):
        q = (hb @ w_query).reshape(PEER_BLOCK, PEER_HEADS, 2, D_HALF).astype(F32)
        s = jnp.einsum('thcd,cnd->thcn', q, sub_keys.astype(F32))
        s1, i1 = lax.top_k(s[:, :, 0], PEER_TOPK)
        s2, i2 = lax.top_k(s[:, :, 1], PEER_TOPK)
        cand = (s1[..., :, None] + s2[..., None, :]).reshape(PEER_BLOCK, PEER_HEADS, PEER_TOPK * PEER_TOPK)
        cidx = (i1[..., :, None] * N_KEYS + i2[..., None, :]).reshape(PEER_BLOCK, PEER_HEADS, PEER_TOPK * PEER_TOPK)
        top, sel = lax.top_k(cand, PEER_TOPK)
        idx = jnp.take_along_axis(cidx, sel, axis=-1).reshape(PEER_BLOCK, PEER_HEADS * PEER_TOPK)
        gate = jax.nn.softmax(top, axis=-1).reshape(PEER_BLOCK, PEER_HEADS * PEER_TOPK)
        pre = jnp.einsum('td,tkd->tk', hb, eu[idx]).astype(F32)
        act = jax.nn.gelu(pre, approximate=False) * gate
        return jnp.einsum('tk,tkd->td', act.astype(hb.dtype), ev[idx])

    out = lax.map(block, blocks).reshape(-1, D_MODEL)[:n]
    return out.reshape(shp)


def trunk_layer(x, pos, prev_h, s0, attend, lp):
    B, T = x.shape[:2]
    h = rmsnorm(x, lp['norm1_g'])
    proj = h @ lp['w_in']
    prev_proj = prev_h.astype(h.dtype) @ lp['w_in'][:, :N_RW_COLS]
    y_rw, s_new = rwkv7_time_mix(proj[..., :N_RW_COLS], prev_proj, s0, lp)
    o = N_RW_COLS
    q = proj[..., o:o + ATTN_WIDTH].reshape(B, T, ATTN_HEADS, HEAD_DIM)
    o += ATTN_WIDTH
    k = proj[..., o:o + KV_HEADS * HEAD_DIM].reshape(B, T, KV_HEADS, HEAD_DIM)
    o += KV_HEADS * HEAD_DIM
    v = proj[..., o:o + KV_HEADS * HEAD_DIM].reshape(B, T, KV_HEADS, HEAD_DIM)
    q = rope_partial(q, pos)
    k = rope_partial(k, pos)
    y_at, k_keep, v_keep = attend(q, k, v, lp['attn_sinks'])
    x = x + jnp.concatenate([y_rw, y_at], axis=-1) @ lp['w_out']
    x = x + peer(rmsnorm(x, lp['norm2_g']), lp)
    return x, (k_keep, v_keep, s_new, h[:, -1])


def setup_inputs(seed: int = 0) -> dict:
    key = jax.random.key(seed)
    ks = jax.random.split(key, 28)
    nrm = lambda k, shp, sc: jax.random.normal(k, shp, F32) * sc
    wb = min(WINDOW, PAST_LEN)
    return {
        'x_prompt': nrm(ks[0], (BATCH, SEQ, D_MODEL), 1.0),
        'x_sample': nrm(ks[1], (DEC_BATCH, DEC_SEQ, D_MODEL), 1.0),
        'cache_k_win': nrm(ks[2], (DEPTH, DEC_BATCH, wb, KV_HEADS, HEAD_DIM), 1.0),
        'cache_v_win': nrm(ks[3], (DEPTH, DEC_BATCH, wb, KV_HEADS, HEAD_DIM), 1.0),
        'state_wkv': nrm(ks[4], (DEPTH, DEC_BATCH, RWKV_HEADS, HEAD_DIM, HEAD_DIM), 0.1),
        'state_shift': nrm(ks[5], (DEPTH, DEC_BATCH, D_MODEL), 1.0),
        'meta_tokens': nrm(ks[6], (N_META, D_MODEL), 1.0),
        'norm1_g': 1.0 + nrm(ks[7], (DEPTH, D_MODEL), 0.02),
        'w_in': nrm(ks[8], (DEPTH, D_MODEL, N_IN_COLS), D_MODEL ** -0.5),
        'mu_shift': jax.random.uniform(ks[9], (DEPTH, N_RW_COLS), F32),
        'w0': jax.random.uniform(ks[10], (DEPTH, RWKV_WIDTH), F32, -6.0, -1.0),
        'w_lora_w2': nrm(ks[11], (DEPTH, DECAY_LORA, RWKV_WIDTH), 0.1),
        'a0': nrm(ks[12], (DEPTH, RWKV_WIDTH), 0.1),
        'w_lora_a2': nrm(ks[13], (DEPTH, AAA_LORA, RWKV_WIDTH), 0.1),
        'w_lora_g2': nrm(ks[14], (DEPTH, GATE_LORA, RWKV_WIDTH), GATE_LORA ** -0.5),
        'k_k': 0.85 + nrm(ks[15], (DEPTH, RWKV_WIDTH), 0.02),
        'k_a': 1.0 + nrm(ks[16], (DEPTH, RWKV_WIDTH), 0.02),
        'r_k': nrm(ks[17], (DEPTH, RWKV_HEADS, HEAD_DIM), 0.1),
        'lnx_w': 1.0 + nrm(ks[18], (DEPTH, RWKV_WIDTH), 0.02),
        'lnx_b': nrm(ks[19], (DEPTH, RWKV_WIDTH), 0.02),
        'attn_sinks': nrm(ks[20], (DEPTH, ATTN_HEADS), 0.5),
        'w_out': nrm(ks[21], (DEPTH, MIX_WIDTH, D_MODEL), MIX_WIDTH ** -0.5),
        'norm2_g': 1.0 + nrm(ks[22], (DEPTH, D_MODEL), 0.02),
        'w_query': nrm(ks[23], (DEPTH, D_MODEL, PEER_HEADS * D_KEY), D_MODEL ** -0.5),
        'sub_keys': nrm(ks[24], (DEPTH, 2, N_KEYS, D_HALF), D_HALF ** -0.5),
        'expert_u': nrm(ks[25], (DEPTH, N_EXPERTS, D_MODEL), D_MODEL ** -0.5),
        'expert_v': nrm(ks[26], (DEPTH, N_EXPERTS, D_MODEL), 0.2),
        'final_norm_g': 1.0 + nrm(ks[27], (D_MODEL,), 0.02),
    }


def reference(x_prompt, x_sample, cache_k_win, cache_v_win, state_wkv, state_shift, meta_tokens, norm1_g, w_in, mu_shift, w0, w_lora_w2, a0, w_lora_a2, w_lora_g2, k_k, k_a, r_k, lnx_w, lnx_b, attn_sinks, w_out, norm2_g, w_query, sub_keys, expert_u, expert_v, final_norm_g):
    B = x_prompt.shape[0]
    meta = jnp.broadcast_to(meta_tokens[None].astype(x_prompt.dtype), (B, N_META, D_MODEL))
    xp = jnp.concatenate([meta, x_prompt], axis=1)
    xs = x_sample
    pos_p = jnp.arange(N_META + x_prompt.shape[1])
    pos_s = PAST_LEN + jnp.arange(x_sample.shape[1])
    new_p = []
    new_s = []
    for l in range(DEPTH):
        lp = dict(norm1_g=norm1_g[l], w_in=w_in[l], mu_shift=mu_shift[l], w0=w0[l], w_lora_w2=w_lora_w2[l],
                  a0=a0[l], w_lora_a2=w_lora_a2[l], w_lora_g2=w_lora_g2[l], k_k=k_k[l], k_a=k_a[l], r_k=r_k[l],
                  lnx_w=lnx_w[l], lnx_b=lnx_b[l], attn_sinks=attn_sinks[l], w_out=w_out[l], norm2_g=norm2_g[l],
                  w_query=w_query[l], sub_keys=sub_keys[l], expert_u=expert_u[l], expert_v=expert_v[l])
        xp, st_p = trunk_layer(xp, pos_p, jnp.zeros((B, D_MODEL), xp.dtype),
                               jnp.zeros((B, RWKV_HEADS, HEAD_DIM, HEAD_DIM), F32), attend_banded, lp)
        xs, st_s = trunk_layer(xs, pos_s, state_shift[l], state_wkv[l],
                               functools.partial(attend_window, k_buf=cache_k_win[l], v_buf=cache_v_win[l]), lp)
        new_p.append(st_p)
        new_s.append(st_s)
    stk = lambda sts, i: jnp.stack([st[i] for st in sts], axis=0)
    y_prompt = rmsnorm(xp, final_norm_g)[:, N_META:]
    y_sample = rmsnorm(xs, final_norm_g)
    return (y_prompt, y_sample,
            stk(new_p, 0), stk(new_p, 1), stk(new_p, 2).astype(state_wkv.dtype), stk(new_p, 3),
            stk(new_s, 0), stk(new_s, 1), stk(new_s, 2).astype(state_wkv.dtype), stk(new_s, 3))
```

```python
import functools

import numpy as np
import jax
import jax.numpy as jnp
from jax import lax
from jax.experimental import pallas as pl
from jax.experimental.pallas import tpu as pltpu

F32 = jnp.float32
BF16 = jnp.bfloat16

D_MODEL = 1024
HEAD_DIM = 64
RWKV_WIDTH = 512
RWKV_HEADS = 8
ATTN_WIDTH = 512
ATTN_HEADS = 8
KV_HEADS = 2
KV_WIDTH = KV_HEADS * HEAD_DIM
DECAY_LORA = 64
AAA_LORA = 64
GATE_LORA = 128
N_RW_COLS = 3 * RWKV_WIDTH + DECAY_LORA + AAA_LORA + GATE_LORA
N_IN_COLS = N_RW_COLS + ATTN_WIDTH + 2 * KV_WIDTH
LNX_EPS = 64e-5
WINDOW = 128
ATT_BLOCK = 128
ATTN_SCALE = HEAD_DIM ** -0.5
ROPE_THETA = 500000.0
ROT_DIM = HEAD_DIM // 4
N_META = 16
N_KEYS = 128
N_EXPERTS = N_KEYS * N_KEYS
PEER_HEADS = 8
PEER_TOPK = 16
N_SEL = PEER_HEADS * PEER_TOPK
D_HALF = 128
NORM_EPS = 1e-5
NEG_INF = -1e30
PAST_LEN = 8192

LANES = 128
SUBLANES = 8
GATE_PITCH = N_KEYS + SUBLANES
VMEM_LIMIT = 56 * 1024 * 1024


def _cparams(n_axes):
    return pltpu.CompilerParams(dimension_semantics=("arbitrary",) * n_axes, vmem_limit_bytes=VMEM_LIMIT)


def _split2(x):
    hi = x.astype(BF16)
    lo = (x - hi.astype(F32)).astype(BF16)
    return hi, lo


def _split3(x):
    hi = x.astype(BF16)
    r1 = x - hi.astype(F32)
    mid = r1.astype(BF16)
    lo = (r1 - mid.astype(F32)).astype(BF16)
    return hi, mid, lo


def _dot(a, b):
    return jnp.dot(a, b, preferred_element_type=F32)


def _dot_nt(a, b):
    return lax.dot_general(a, b, (((1,), (1,)), ((), ())), preferred_element_type=F32)


def _dot_exact01(x, m01):
    hi, mid, lo = _split3(x)
    return _dot(hi, m01) + _dot(mid, m01) + _dot(lo, m01)


def _dot3(x, wh, wl):
    xh, xl = _split2(x)
    return _dot(xh, wh) + (_dot(xl, wh) + _dot(xh, wl))


def _dot3_nt(x, y):
    xh, xl = _split2(x)
    yh, yl = _split2(y)
    return _dot_nt(xh, yh) + (_dot_nt(xl, yh) + _dot_nt(xh, yl))


def _rms(x, g):
    return x * lax.rsqrt(jnp.mean(x * x, axis=-1, keepdims=True) + NORM_EPS) * g


def _inproj_body(x_ref, g_ref, w_ref, o_ref, *, norm):
    x = x_ref[...]
    if norm:
        x = _rms(x, g_ref[...])
    o_ref[...] = _dot(x.astype(BF16), w_ref[...])


def _inproj(x, g, w_bf, *, norm, tm):
    rows, ncol = x.shape[0], w_bf.shape[1]
    return pl.pallas_call(
        functools.partial(_inproj_body, norm=norm),
        grid=(rows // tm,),
        in_specs=[pl.BlockSpec((tm, D_MODEL), lambda i: (i, 0)),
                  pl.BlockSpec((1, D_MODEL), lambda i: (0, 0)),
                  pl.BlockSpec((D_MODEL, ncol), lambda i: (0, 0))],
        out_specs=pl.BlockSpec((tm, ncol), lambda i: (i, 0)),
        out_shape=jax.ShapeDtypeStruct((rows, ncol), F32),
        compiler_params=_cparams(1), name="inproj")(x, g, w_bf)


def _rmsnorm_body(x_ref, g_ref, o_ref):
    o_ref[...] = _rms(x_ref[...], g_ref[...])


def _rmsnorm_rows(x, g):
    rows = x.shape[0]
    return pl.pallas_call(
        _rmsnorm_body, grid=(1,),
        in_specs=[pl.BlockSpec((rows, D_MODEL), lambda i: (0, 0)), pl.BlockSpec((1, D_MODEL), lambda i: (0, 0))],
        out_specs=pl.BlockSpec((rows, D_MODEL), lambda i: (0, 0)),
        out_shape=jax.ShapeDtypeStruct((rows, D_MODEL), F32),
        compiler_params=_cparams(1), name="rmsnorm_rows")(x, g)


def _rope(x, c, s_up, s_dn):
    up = pltpu.roll(x, LANES - ROT_DIM // 2, 1)
    dn = pltpu.roll(x, ROT_DIM // 2, 1)
    return x * c + up * s_up + dn * s_dn


def _prep_body(*refs, seq_rows, tile_rows, sample):
    if sample:
        p_ref, prev_ref = refs[0], refs[1]
    else:
        p_ref, prev8_ref = refs[0], refs[1]
    (cos_ref, sup_ref, sdn_ref, mu_ref, w0_ref, a0_ref, kk_ref, ka_ref, rk_ref,
     w2h_ref, w2l_ref, a2h_ref, a2l_ref, g2h_ref, g2l_ref, bd_ref) = refs[2:18]
    r_o, dec_o, k_o, v_o, av_o, bv_o, g_o, bon_o, q_o, ka_o, va_o = refs[18:]

    prw = p_ref[:, :N_RW_COLS]
    rolled = pltpu.roll(prw, 1, 0)
    row = lax.broadcasted_iota(jnp.int32, (tile_rows, 1), 0)
    if sample:
        pprev = jnp.where(row % seq_rows == 0, prev_ref[:, :N_RW_COLS], rolled)
    else:
        tiles_per_seq = seq_rows // tile_rows
        at_start = (pl.program_id(0) % tiles_per_seq) == 0
        first = jnp.where(at_start, 0.0, prev8_ref[SUBLANES - 1:SUBLANES, :N_RW_COLS])
        pprev = jnp.where(row == 0, first, rolled)
    m = prw + (pprev - prw) * mu_ref[...]

    c = RWKV_WIDTH
    xr, xk, xv = m[:, :c], m[:, c:2 * c], m[:, 2 * c:3 * c]
    xwa = m[:, 3 * c:3 * c + LANES]
    xg = m[:, 3 * c + LANES:3 * c + 2 * LANES]

    lw = w0_ref[...] + _dot3(jnp.tanh(xwa), w2h_ref[...], w2l_ref[...])
    z = -lw
    w = -(jnp.maximum(z, 0.0) + jnp.log1p(jnp.exp(-jnp.abs(z)))) - 0.5
    dec_o[...] = jnp.exp(-jnp.exp(w))
    a = jax.nn.sigmoid(a0_ref[...] + _dot3(xwa, a2h_ref[...], a2l_ref[...]))
    g_o[...] = _dot3(jax.nn.sigmoid(xg), g2h_ref[...], g2l_ref[...])

    bd = bd_ref[...]
    kk = xk * kk_ref[...]
    kk = kk / jnp.maximum(jnp.sqrt(_dot_exact01(kk * kk, bd)), 1e-12)
    kmod = xk * (1.0 + (a - 1.0) * ka_ref[...])
    r_o[...] = xr
    k_o[...] = kmod
    v_o[...] = xv
    av_o[...] = -kk
    bv_o[...] = kk * a
    bon_o[...] = _dot_exact01(xr * kmod * rk_ref[...], bd) * xv

    cs, su, sd = cos_ref[...], sup_ref[...], sdn_ref[...]
    o = N_RW_COLS
    for s in range(ATTN_WIDTH // LANES):
        q_o[:, s * LANES:(s + 1) * LANES] = _rope(p_ref[:, o + s * LANES:o + (s + 1) * LANES], cs, su, sd) * ATTN_SCALE
    o += ATTN_WIDTH
    ka_o[...] = _rope(p_ref[:, o:o + KV_WIDTH], cs, su, sd)
    va_o[...] = p_ref[:, o + KV_WIDTH:o + 2 * KV_WIDTH]


def _prep(p, prev, tabs, params, *, seq_rows, tm, sample):
    rows = p.shape[0]
    n_tiles = rows // tm
    row_spec = lambda w: pl.BlockSpec((tm, w), lambda i: (i, 0))
    const = lambda a: pl.BlockSpec(a.shape, lambda i: (0,) * a.ndim)
    if sample:
        prev_spec = row_spec(N_IN_COLS)
        tab_spec = row_spec(LANES)
    else:
        blocks = tm // SUBLANES
        prev_spec = pl.BlockSpec((SUBLANES, N_IN_COLS), lambda i: (jnp.maximum(i * blocks - 1, 0), 0))
        tiles_per_seq = seq_rows // tm
        tab_spec = pl.BlockSpec((tm, LANES), lambda i: (i % tiles_per_seq, 0))
    in_specs = [row_spec(N_IN_COLS), prev_spec, tab_spec, tab_spec, tab_spec] + [const(a) for a in params]
    widths = [RWKV_WIDTH] * 8 + [ATTN_WIDTH, KV_WIDTH, KV_WIDTH]
    return pl.pallas_call(
        functools.partial(_prep_body, seq_rows=seq_rows, tile_rows=tm, sample=sample),
        grid=(n_tiles,), in_specs=in_specs,
        out_specs=[row_spec(w) for w in widths],
        out_shape=[jax.ShapeDtypeStruct((rows, w), F32) for w in widths],
        compiler_params=_cparams(1), name="prep")(p, prev, *tabs, *params)


def _scan_body(r_ref, w_ref, k_ref, v_ref, a_ref, b_ref, s0_ref, y_ref, so_ref,
               r_s, w_s, k_s, v_s, a_s, b_s, y_s, st_s, *, nb, tc):
    ci = pl.program_id(1)

    @pl.when(ci == 0)
    def _():
        st_s[...] = s0_ref[...]

    for src, dst in ((r_ref, r_s), (w_ref, w_s), (k_ref, k_s), (v_ref, v_s), (a_ref, a_s), (b_ref, b_s)):
        for b in range(nb):
            for h in range(RWKV_HEADS):
                dst[b * RWKV_HEADS + h] = src[b, :, h * HEAD_DIM:(h + 1) * HEAD_DIM]

    eye = (lax.broadcasted_iota(jnp.int32, (HEAD_DIM, HEAD_DIM), 0)
           == lax.broadcasted_iota(jnp.int32, (HEAD_DIM, HEAD_DIM), 1)).astype(F32)

    def step(t, carry):
        for s in range(nb * RWKV_HEADS):
            row = lambda ref: ref[s, pl.ds(t, 1), :]
            st = st_s[s]
            sa = jnp.sum(st * row(a_s), axis=1, keepdims=True)
            vcol = jnp.sum(eye * row(v_s), axis=1, keepdims=True)
            st = st * row(w_s) + sa * row(b_s) + vcol * row(k_s)
            st_s[s] = st
            ycol = jnp.sum(st * row(r_s), axis=1, keepdims=True)
            y_s[s, pl.ds(t, 1), :] = jnp.sum(ycol * eye, axis=0, keepdims=True)
        return carry

    lax.fori_loop(0, tc, step, 0)

    for b in range(nb):
        for h in range(RWKV_HEADS):
            y_ref[b, :, h * HEAD_DIM:(h + 1) * HEAD_DIM] = y_s[b * RWKV_HEADS + h]

    @pl.when(ci == pl.num_programs(1) - 1)
    def _():
        so_ref[...] = st_s[...]


def _scan(r, w, k, v, a, b, s0, *, nb, tc):
    batch, t_len = r.shape[0], r.shape[1]
    ns = nb * RWKV_HEADS
    seq_spec = pl.BlockSpec((nb, tc, RWKV_WIDTH), lambda g, c: (g, c, 0))
    st_spec = pl.BlockSpec((ns, HEAD_DIM, HEAD_DIM), lambda g, c: (g, 0, 0))
    head_scr = pltpu.VMEM((ns, tc, HEAD_DIM), F32)
    return pl.pallas_call(
        functools.partial(_scan_body, nb=nb, tc=tc),
        grid=(batch // nb, t_len // tc),
        in_specs=[seq_spec] * 6 + [st_spec],
        out_specs=[seq_spec, st_spec],
        out_shape=[jax.ShapeDtypeStruct(r.shape, F32), jax.ShapeDtypeStruct(s0.shape, F32)],
        scratch_shapes=[head_scr] * 7 + [pltpu.VMEM((ns, HEAD_DIM, HEAD_DIM), F32)],
        compiler_params=_cparams(2), name="wkv_scan")(r, w, k, v, a, b, s0)


def _sink_softmax(sc, valid, sink):
    sm = jnp.where(valid, sc, NEG_INF)
    mx = jnp.maximum(jnp.max(sm, axis=-1, keepdims=True), sink)
    p = jnp.exp(sm - mx)
    return p / (jnp.sum(p, axis=-1, keepdims=True) + jnp.exp(sink - mx))


def _attn_prompt_body(sink_ref, q_ref, kp_ref, kc_ref, vp_ref, vc_ref, o_ref, *, front):
    n = pl.program_id(1)
    keys = jnp.concatenate([kp_ref[...], kc_ref[...]], axis=0)
    vals = jnp.concatenate([vp_ref[...], vc_ref[...]], axis=0)
    shp = (ATT_BLOCK, 2 * ATT_BLOCK)
    qpos = n * ATT_BLOCK - front + lax.broadcasted_iota(jnp.int32, shp, 0)
    kpos = (n - 1) * ATT_BLOCK - front + lax.broadcasted_iota(jnp.int32, shp, 1)
    dlt = qpos - kpos
    valid = (kpos >= 0) & (dlt >= 0) & (dlt < WINDOW)
    half = lax.broadcasted_iota(jnp.int32, (ATT_BLOCK, LANES), 1) // HEAD_DIM
    for s in range(ATTN_WIDTH // LANES):
        q = q_ref[:, s * LANES:(s + 1) * LANES]
        acc = jnp.zeros((ATT_BLOCK, LANES), F32)
        for g in range(KV_HEADS):
            sc = _dot3_nt(jnp.where(half == g, q, 0.0), keys)
            p = _sink_softmax(sc, valid, sink_ref[s + g * (ATTN_HEADS // KV_HEADS)])
            ph, plo = _split2(p)
            vh, vl = _split2(vals)
            o = _dot(ph, vh) + (_dot(plo, vh) + _dot(ph, vl))
            acc = acc + jnp.where(half == g, o, 0.0)
        o_ref[:, s * LANES:(s + 1) * LANES] = acc


def _attn_prompt(sinks, q, k, v, *, front):
    batch, t_len = q.shape[0], q.shape[1]
    blk = lambda w, f: pl.BlockSpec((None, ATT_BLOCK, w), f)
    cur = lambda b, n: (b, n, 0)
    prv = lambda b, n: (b, jnp.maximum(n - 1, 0), 0)
    return pl.pallas_call(
        functools.partial(_attn_prompt_body, front=front),
        grid=(batch, t_len // ATT_BLOCK),
        in_specs=[pl.BlockSpec(memory_space=pltpu.SMEM), blk(ATTN_WIDTH, cur),
                  blk(KV_WIDTH, prv), blk(KV_WIDTH, cur), blk(KV_WIDTH, prv), blk(KV_WIDTH, cur)],
        out_specs=blk(ATTN_WIDTH, cur),
        out_shape=jax.ShapeDtypeStruct(q.shape, F32),
        compiler_params=_cparams(2), name="attn_prompt")(sinks, q, k, k, v, v)


def _attn_sample_body(sink_ref, q_ref, kn_ref, vn_ref, kb_ref, vb_ref, o_ref, *, bb, n_new):
    keys = jnp.concatenate([kb_ref[...], kn_ref[...]], axis=1)
    vals = jnp.concatenate([vb_ref[...], vn_ref[...]], axis=1)
    n_keys = WINDOW + SUBLANES
    shp = (n_new, n_keys)
    tq = lax.broadcasted_iota(jnp.int32, shp, 0)
    sk = lax.broadcasted_iota(jnp.int32, shp, 1)
    dlt = tq - (sk - WINDOW)
    valid = ((dlt >= 0) & (dlt < WINDOW))[None]
    half = lax.broadcasted_iota(jnp.int32, (bb, n_new, LANES), 2) // HEAD_DIM
    bdot = lambda x, y, cd: lax.dot_general(x, y, ((cd, ((0,), (0,)))), preferred_element_type=F32)
    kh, kl = _split2(keys)
    vh, vl = _split2(vals)
    for s in range(ATTN_WIDTH // LANES):
        q = q_ref[:, :, s * LANES:(s + 1) * LANES]
        acc = jnp.zeros((bb, n_new, LANES), F32)
        for g in range(KV_HEADS):
            qh, ql = _split2(jnp.where(half == g, q, 0.0))
            nt = ((2,), (2,))
            sc = bdot(qh, kh, nt) + (bdot(ql, kh, nt) + bdot(qh, kl, nt))
            p = _sink_softmax(sc, valid, sink_ref[s + g * (ATTN_HEADS // KV_HEADS)])
            ph, plo = _split2(p)
            nn = ((2,), (1,))
            o = bdot(ph, vh, nn) + (bdot(plo, vh, nn) + bdot(ph, vl, nn))
            acc = acc + jnp.where(half == g, o, 0.0)
        o_ref[:, :, s * LANES:(s + 1) * LANES] = acc


def _attn_sample(sinks, q, k_new, v_new, k_buf, v_buf, *, bb):
    batch, n_new = q.shape[0], q.shape[1]
    spec = lambda a: pl.BlockSpec((bb,) + a.shape[1:], lambda i: (i, 0, 0))
    return pl.pallas_call(
        functools.partial(_attn_sample_body, bb=bb, n_new=n_new),
        grid=(batch // bb,),
        in_specs=[pl.BlockSpec(memory_space=pltpu.SMEM)] + [spec(a) for a in (q, k_new, v_new, k_buf, v_buf)],
        out_specs=spec(q),
        out_shape=jax.ShapeDtypeStruct(q.shape, F32),
        compiler_params=_cparams(1), name="attn_sample")(sinks, q, k_new, v_new, k_buf, v_buf)


def _mix_body(x_ref, y_ref, g_ref, bon_ref, ya_ref, lw_ref, lb_ref, bd_ref, wo_ref, n2_ref, wq_ref,
              skh_ref, skl_ref, x1_o, h2_o, sc_o):
    bd = bd_ref[...]
    y = y_ref[...]
    inv = 1.0 / HEAD_DIM
    mean = _dot_exact01(y, bd) * inv
    yc = y - mean
    var = _dot_exact01(yc * yc, bd) * inv
    yn = yc * lax.rsqrt(var + LNX_EPS) * lw_ref[...] + lb_ref[...]
    yrw = (yn + bon_ref[...]) * g_ref[...]
    x1 = x_ref[...] + (_dot(yrw.astype(BF16), wo_ref[:RWKV_WIDTH, :]) + _dot(ya_ref[...].astype(BF16), wo_ref[RWKV_WIDTH:, :]))
    x1_o[...] = x1
    h2 = _rms(x1, n2_ref[...]).astype(BF16)
    h2_o[...] = h2
    qp = _dot(h2, wq_ref[...])
    for hc in range(2 * PEER_HEADS):
        c = hc % 2
        qs = qp[:, hc * D_HALF:(hc + 1) * D_HALF]
        qh, ql = _split2(qs)
        sc_o[:, hc * N_KEYS:(hc + 1) * N_KEYS] = (
            _dot_nt(qh, skh_ref[c]) + (_dot_nt(ql, skh_ref[c]) + _dot_nt(qh, skl_ref[c])))


def _mix(x, y, g, bon, ya, params, *, tm, skip):
    batch, t_len = x.shape[0], x.shape[1]
    n_out = t_len // tm - skip
    src = lambda w: pl.BlockSpec((None, tm, w), lambda b, j: (b, j + skip, 0))
    dst = lambda w: pl.BlockSpec((None, tm, w), lambda b, j: (b, j, 0))
    const = lambda a: pl.BlockSpec(a.shape, lambda b, j: (0,) * a.ndim)
    n_sc = 2 * PEER_HEADS * N_KEYS
    return pl.pallas_call(
        _mix_body, grid=(batch, n_out),
        in_specs=[src(D_MODEL), src(RWKV_WIDTH), src(RWKV_WIDTH), src(RWKV_WIDTH), src(ATTN_WIDTH)]
        + [const(a) for a in params],
        out_specs=[dst(D_MODEL), dst(D_MODEL), dst(n_sc)],
        out_shape=[jax.ShapeDtypeStruct((batch, n_out * tm, D_MODEL), F32),
                   jax.ShapeDtypeStruct((batch, n_out * tm, D_MODEL), BF16),
                   jax.ShapeDtypeStruct((batch, n_out * tm, n_sc), F32)],
        compiler_params=_cparams(2), name="mix")(x, y, g, bon, ya, *params)


def _topk_body(s_ref, e1_ref, e2_ref, bd_ref, ia_o, ib_o, gt_o, mx_s, *, tm):
    h = pl.program_id(1)
    lane = lax.broadcasted_iota(jnp.int32, (tm, LANES), 1).astype(F32)
    lane2 = lax.broadcasted_iota(jnp.int32, (tm, 2 * LANES), 1).astype(F32)
    hf = h.astype(F32)

    @pl.when(h == 0)
    def _():
        ia_o[...] = jnp.zeros((tm, LANES), jnp.int32)
        ib_o[...] = jnp.zeros((tm, LANES), jnp.int32)
        gt_o[...] = jnp.zeros((tm, LANES), F32)
        mx_s[...] = jnp.zeros((tm, LANES), F32)

    def top16(x):
        def body(it, c):
            x, vals, idxs = c
            m = jnp.max(x, axis=1, keepdims=True)
            am = jnp.min(jnp.where(x == m, lane, float(LANES)), axis=1, keepdims=True)
            rec = lane == it.astype(F32)
            return jnp.where(lane == am, -jnp.inf, x), jnp.where(rec, m, vals), jnp.where(rec, am, idxs)
        z = jnp.zeros((tm, LANES), F32)
        _, vals, idxs = lax.fori_loop(0, PEER_TOPK, body, (x, z, z))
        return vals, idxs

    v1, i1 = top16(s_ref[:, :N_KEYS])
    v2, i2 = top16(s_ref[:, N_KEYS:])
    e1, e2 = e1_ref[...], e2_ref[...]
    cand = _dot_exact01(v1, e1) + _dot_exact01(v2, e2)
    aexp = _dot(i1.astype(BF16), e1)
    bexp = _dot(i2.astype(BF16), e2)

    def body2(it, c):
        cand, top, ia, ib = c
        m = jnp.max(cand, axis=1, keepdims=True)
        am = jnp.min(jnp.where(cand == m, lane2, float(2 * LANES)), axis=1, keepdims=True)
        sel = lane2 == am
        av = jnp.sum(jnp.where(sel, aexp, 0.0), axis=1, keepdims=True)
        bv = jnp.sum(jnp.where(sel, bexp, 0.0), axis=1, keepdims=True)
        rec = lane == hf * PEER_TOPK + it.astype(F32)
        return (jnp.where(sel, -jnp.inf, cand), jnp.where(rec, m, top),
                jnp.where(rec, av.astype(jnp.int32), ia), jnp.where(rec, bv.astype(jnp.int32), ib))

    _, top, ia, ib = lax.fori_loop(0, PEER_TOPK, body2, (cand, gt_o[...], ia_o[...], ib_o[...]))
    ia_o[...] = ia
    ib_o[...] = ib
    first = jnp.max(cand, axis=1, keepdims=True)
    mine = (lane >= hf * PEER_TOPK) & (lane < (hf + 1.0) * PEER_TOPK)
    mx = jnp.where(mine, first, mx_s[...])
    mx_s[...] = mx

    @pl.when(h < PEER_HEADS - 1)
    def _():
        gt_o[...] = top

    @pl.when(h == PEER_HEADS - 1)
    def _():
        e = jnp.exp(top - mx)
        gt_o[...] = e / _dot_exact01(e, bd_ref[...])


def _topk(scores, e1, e2, bd16, *, tm):
    rows = scores.shape[0]
    out = pl.BlockSpec((tm, LANES), lambda i, h: (i, 0))
    const = lambda a: pl.BlockSpec(a.shape, lambda i, h: (0,) * a.ndim)
    return pl.pallas_call(
        functools.partial(_topk_body, tm=tm), grid=(rows // tm, PEER_HEADS),
        in_specs=[pl.BlockSpec((tm, 2 * N_KEYS), lambda i, h: (i, h)), const(e1), const(e2), const(bd16)],
        out_specs=[out, out, out],
        out_shape=[jax.ShapeDtypeStruct((rows, LANES), jnp.int32), jax.ShapeDtypeStruct((rows, LANES), jnp.int32),
                   jax.ShapeDtypeStruct((rows, LANES), F32)],
        scratch_shapes=[pltpu.VMEM((tm, LANES), F32)],
        compiler_params=_cparams(2), name="peer_topk")(scores, e1, e2, bd16)


def _peer_body(h_ref, ia_ref, ib_ref, gt_ref, x1_ref, fg_ref, eu_ref, ev_ref, o_ref, gate_s, acc_s, *, tm, n_a):
    e = pl.program_id(1)

    @pl.when(e == 0)
    def _():
        acc_s[...] = jnp.zeros((tm, D_MODEL), F32)
        sub = lax.broadcasted_iota(jnp.int32, (N_KEYS, N_SEL), 0)

        def build(t, c):
            pa = jnp.where(sub == ia_ref[pl.ds(t, 1), :], gt_ref[pl.ds(t, 1), :], 0.0).astype(BF16)
            qb = jnp.where(sub == ib_ref[pl.ds(t, 1), :], 1.0, 0.0).astype(BF16)
            gate_s[pl.ds(pl.multiple_of(t * GATE_PITCH, SUBLANES), N_KEYS), :] = _dot_nt(pa, qb)
            return c

        lax.fori_loop(0, tm, build, 0)

    pre = _dot(h_ref[...], eu_ref[...])
    acts = []
    for ai in range(n_a):
        x = pre[:, ai * N_KEYS:(ai + 1) * N_KEYS]
        gate = gate_s[pl.ds(e * n_a + ai, tm, stride=GATE_PITCH), :]
        acts.append((0.5 * x * (1.0 + lax.erf(x * np.float32(np.sqrt(0.5)))) * gate).astype(BF16))
    acc_s[...] += _dot(jnp.concatenate(acts, axis=1), ev_ref[...])

    @pl.when(e == pl.num_programs(1) - 1)
    def _():
        o_ref[...] = _rms(x1_ref[...] + acc_s[...], fg_ref[...])


def _peer(h2, ia, ib, gt, x1, fg, eu_t, ev, *, tm, n_a):
    rows = h2.shape[0]
    eb = n_a * N_KEYS
    tok = lambda w: pl.BlockSpec((tm, w), lambda i, e: (i, 0))
    return pl.pallas_call(
        functools.partial(_peer_body, tm=tm, n_a=n_a),
        grid=(rows // tm, N_EXPERTS // eb),
        in_specs=[tok(D_MODEL), tok(N_SEL), tok(N_SEL), tok(N_SEL), tok(D_MODEL),
                  pl.BlockSpec((1, D_MODEL), lambda i, e: (0, 0)),
                  pl.BlockSpec((D_MODEL, eb), lambda i, e: (0, e)),
                  pl.BlockSpec((eb, D_MODEL), lambda i, e: (e, 0))],
        out_specs=tok(D_MODEL),
        out_shape=jax.ShapeDtypeStruct((rows, D_MODEL), F32),
        scratch_shapes=[pltpu.VMEM((tm * GATE_PITCH, N_KEYS), F32), pltpu.VMEM((tm, D_MODEL), F32)],
        compiler_params=_cparams(2), name="peer_experts")(h2, ia, ib, gt, x1, fg, eu_t, ev)


def _rope_tables(pos):
    half = ROT_DIM // 2
    inv_freq = ROPE_THETA ** (-jnp.arange(0, ROT_DIM, 2, dtype=F32) / ROT_DIM)
    ang = pos.astype(F32)[:, None] * inv_freq[None, :]
    cos, sin = jnp.cos(ang), jnp.sin(ang)
    n = pos.shape[0]
    ones = jnp.ones((n, HEAD_DIM - ROT_DIM), F32)
    zeros = jnp.zeros((n, HEAD_DIM - ROT_DIM), F32)
    zh = jnp.zeros((n, half), F32)
    c = jnp.concatenate([cos, cos, ones], axis=1)
    s_up = jnp.concatenate([-sin, zh, zeros], axis=1)
    s_dn = jnp.concatenate([zh, sin, zeros], axis=1)
    rep = LANES // HEAD_DIM
    return tuple(jnp.tile(t, (1, rep)) for t in (c, s_up, s_dn))


def _block_ones(n, blk):
    i = np.arange(n) // blk
    return jnp.asarray((i[:, None] == i[None, :]).astype(np.float32), dtype=BF16)


def _pair_expanders():
    k = np.arange(PEER_TOPK * PEER_TOPK)
    r = np.arange(LANES)
    e1 = (r[:, None] == (k // PEER_TOPK)[None, :]).astype(np.float32)
    e2 = (r[:, None] == (k % PEER_TOPK)[None, :]).astype(np.float32)
    return jnp.asarray(e1, dtype=BF16), jnp.asarray(e2, dtype=BF16)


def _hi_lo(w):
    hi = w.astype(BF16)
    return hi, (w - hi.astype(F32)).astype(BF16)


def kernel(x_prompt, x_sample, cache_k_win, cache_v_win, state_wkv, state_shift, meta_tokens, norm1_g, w_in, mu_shift, w0, w_lora_w2, a0, w_lora_a2, w_lora_g2, k_k, k_a, r_k, lnx_w, lnx_b, attn_sinks, w_out, norm2_g, w_query, sub_keys, expert_u, expert_v, final_norm_g):
    batch, seq = x_prompt.shape[0], x_prompt.shape[1]
    dec_batch, dec_seq = x_sample.shape[0], x_sample.shape[1]
    l_real = N_META + seq
    front = (-l_real) % ATT_BLOCK
    l_pad = l_real + front
    head_rows = front + N_META
    assert head_rows % ATT_BLOCK == 0 and dec_seq <= SUBLANES and cache_k_win.shape[2] == WINDOW

    hpg = ATTN_HEADS // KV_HEADS
    qperm = np.concatenate([(s + hpg * g) * HEAD_DIM + np.arange(HEAD_DIM) for s in range(hpg) for g in range(KV_HEADS)])
    win = w_in[0]
    win = jnp.concatenate([win[:, :N_RW_COLS], win[:, N_RW_COLS + qperm], win[:, N_RW_COLS + ATTN_WIDTH:]], axis=1).astype(BF16)
    wout = w_out[0]
    wout = jnp.concatenate([wout[:RWKV_WIDTH], wout[RWKV_WIDTH + qperm]], axis=0).astype(BF16)
    sinks = attn_sinks[0]

    row = lambda a: a.reshape(1, -1)
    zpad = lambda w, top: jnp.concatenate([w, jnp.zeros_like(w)] if top else [jnp.zeros_like(w), w], axis=0)
    bd64 = _block_ones(RWKV_WIDTH, HEAD_DIM)
    prep_params = (row(mu_shift[0]), row(w0[0]), row(a0[0]), row(k_k[0]), row(k_a[0]), row(r_k[0]),
                   *_hi_lo(zpad(w_lora_w2[0], True)), *_hi_lo(zpad(w_lora_a2[0], False)), *_hi_lo(w_lora_g2[0]), bd64)
    mix_params = (row(lnx_w[0]), row(lnx_b[0]), bd64, wout, row(norm2_g[0]), w_query[0].astype(BF16), *_hi_lo(sub_keys[0]))
    e1, e2 = _pair_expanders()
    bd16 = _block_ones(N_SEL, PEER_TOPK)
    eu_t = expert_u[0].astype(BF16).T
    ev = expert_v[0].astype(BF16)
    g1, fg = row(norm1_g[0]), row(final_norm_g)

    def ffn(x1, h2, scores):
        rows = x1.shape[0] * x1.shape[1]
        ia, ib, gt = _topk(scores.reshape(rows, -1), e1, e2, bd16, tm=256)
        return _peer(h2.reshape(rows, D_MODEL), ia, ib, gt, x1.reshape(rows, D_MODEL), fg, eu_t, ev, tm=256, n_a=4)

    xp = jnp.concatenate([jnp.zeros((batch, front, D_MODEL), F32),
                          jnp.broadcast_to(meta_tokens[None], (batch, N_META, D_MODEL)), x_prompt], axis=1)
    proj = _inproj(xp.reshape(batch * l_pad, D_MODEL), g1, win, norm=True, tm=256)
    tabs = _rope_tables(jnp.arange(l_pad) - front)
    r, dec, kx, vx, av, bv, gg, bon, q, ka, va = _prep(proj, proj, tabs, prep_params, seq_rows=l_pad, tm=384, sample=False)
    b3 = lambda a: a.reshape(batch, l_pad, -1)
    y_rw, s_p = _scan(b3(r), b3(dec), b3(kx), b3(vx), b3(av), b3(bv),
                      jnp.zeros((batch * RWKV_HEADS, HEAD_DIM, HEAD_DIM), F32), nb=batch, tc=64)
    y_at = _attn_prompt(sinks, b3(q), b3(ka), b3(va), front=front)
    x1, h2, scores = _mix(xp, y_rw, b3(gg), b3(bon), y_at, mix_params, tm=ATT_BLOCK, skip=head_rows // ATT_BLOCK)
    y_prompt = ffn(x1, h2, scores).reshape(batch, seq, D_MODEL)
    keep = min(WINDOW, l_real)
    k_win_p = b3(ka)[:, l_pad - keep:].reshape(1, batch, keep, KV_HEADS, HEAD_DIM)
    v_win_p = b3(va)[:, l_pad - keep:].reshape(1, batch, keep, KV_HEADS, HEAD_DIM)
    wkv_p = s_p.reshape(1, batch, RWKV_HEADS, HEAD_DIM, HEAD_DIM)

    n_s = dec_batch * dec_seq
    xs = x_sample.reshape(n_s, D_MODEL)
    proj_s = _inproj(xs, g1, win, norm=True, tm=256)
    prev_proj = _inproj(state_shift[0], g1, win, norm=False, tm=dec_batch)
    prev_rows = jnp.pad(prev_proj[:, None, :], ((0, 0), (0, dec_seq - 1), (0, 0))).reshape(n_s, N_IN_COLS)
    pos_s = jnp.tile(PAST_LEN + jnp.arange(dec_seq), dec_batch)
    r, dec, kx, vx, av, bv, gg, bon, q, ka, va = _prep(proj_s, prev_rows, _rope_tables(pos_s), prep_params,
                                                       seq_rows=dec_seq, tm=n_s, sample=True)
    s3 = lambda a: a.reshape(dec_batch, dec_seq, -1)
    y_rw, s_s = _scan(s3(r), s3(dec), s3(kx), s3(vx), s3(av), s3(bv),
                      state_wkv[0].reshape(dec_batch * RWKV_HEADS, HEAD_DIM, HEAD_DIM), nb=4, tc=dec_seq)
    k_buf = cache_k_win[0].reshape(dec_batch, WINDOW, KV_WIDTH)
    v_buf = cache_v_win[0].reshape(dec_batch, WINDOW, KV_WIDTH)
    pad8 = lambda a: jnp.pad(a, ((0, 0), (0, SUBLANES - dec_seq), (0, 0)))
    y_at = _attn_sample(sinks, s3(q), pad8(s3(ka)), pad8(s3(va)), k_buf, v_buf, bb=8)
    one = lambda a: a.reshape(1, n_s, -1)
    x1, h2, scores = _mix(one(xs), one(y_rw), one(gg), one(bon), one(y_at), mix_params, tm=ATT_BLOCK, skip=0)
    y_sample = ffn(x1, h2, scores).reshape(dec_batch, dec_seq, D_MODEL)
    k_win_s = jnp.concatenate([k_buf, s3(ka)], axis=1)[:, -WINDOW:].reshape(1, dec_batch, WINDOW, KV_HEADS, HEAD_DIM)
    v_win_s = jnp.concatenate([v_buf, s3(va)], axis=1)[:, -WINDOW:].reshape(1, dec_batch, WINDOW, KV_HEADS, HEAD_DIM)
    wkv_s = s_s.reshape(1, dec_batch, RWKV_HEADS, HEAD_DIM, HEAD_DIM)

    last = jnp.concatenate([x_prompt[:, -1], x_sample[:, -1]], axis=0)
    pad_rows = (-last.shape[0]) % SUBLANES
    shift = _rmsnorm_rows(jnp.pad(last, ((0, pad_rows), (0, 0))), g1)
    shift_p = shift[:batch].reshape(1, batch, D_MODEL)
    shift_s = shift[batch:batch + dec_batch].reshape(1, dec_batch, D_MODEL)

    return (y_prompt, y_sample, k_win_p, v_win_p, wkv_p, shift_p, k_win_s, v_win_s, wkv_s, shift_s)
```

```python
import functools

import numpy as np
import jax
import jax.numpy as jnp
from jax import lax
from jax.experimental import pallas as pl
from jax.experimental.pallas import tpu as pltpu

F32 = jnp.float32
BF16 = jnp.bfloat16

D_MODEL = 1024
HEAD_DIM = 64
RWKV_WIDTH = 512
RWKV_HEADS = 8
ATTN_WIDTH = 512
ATTN_HEADS = 8
KV_HEADS = 2
KV_WIDTH = KV_HEADS * HEAD_DIM
DECAY_LORA = 64
AAA_LORA = 64
GATE_LORA = 128
N_RW_COLS = 3 * RWKV_WIDTH + DECAY_LORA + AAA_LORA + GATE_LORA
N_IN_COLS = N_RW_COLS + ATTN_WIDTH + 2 * KV_WIDTH
LNX_EPS = 64e-5
WINDOW = 128
ATT_BLOCK = 128
ATTN_SCALE = HEAD_DIM ** -0.5
ROPE_THETA = 500000.0
ROT_DIM = HEAD_DIM // 4
N_META = 16
N_KEYS = 128
N_EXPERTS = N_KEYS * N_KEYS
PEER_HEADS = 8
PEER_TOPK = 16
N_SEL = PEER_HEADS * PEER_TOPK
D_HALF = 128
NORM_EPS = 1e-5
NEG_INF = -1e30
PAST_LEN = 8192

LANES = 128
SUBLANES = 8
GATE_PITCH = N_KEYS + SUBLANES
VMEM_LIMIT = 56 * 1024 * 1024
GATE_BUILD_UNROLL = 8
PEER_TOKEN_TILE = 256


def _cparams(n_axes):
    return pltpu.CompilerParams(dimension_semantics=("arbitrary",) * n_axes, vmem_limit_bytes=VMEM_LIMIT)


def _split2(x):
    hi = x.astype(BF16)
    lo = (x - hi.astype(F32)).astype(BF16)
    return hi, lo


def _split3(x):
    hi = x.astype(BF16)
    r1 = x - hi.astype(F32)
    mid = r1.astype(BF16)
    lo = (r1 - mid.astype(F32)).astype(BF16)
    return hi, mid, lo


def _dot(a, b):
    return jnp.dot(a, b, preferred_element_type=F32)


def _dot_nt(a, b):
    return lax.dot_general(a, b, (((1,), (1,)), ((), ())), preferred_element_type=F32)


def _dot_exact01(x, m01):
    hi, mid, lo = _split3(x)
    return _dot(hi, m01) + _dot(mid, m01) + _dot(lo, m01)


def _dot3(x, wh, wl):
    xh, xl = _split2(x)
    return _dot(xh, wh) + (_dot(xl, wh) + _dot(xh, wl))


def _dot3_nt(x, y):
    xh, xl = _split2(x)
    yh, yl = _split2(y)
    return _dot_nt(xh, yh) + (_dot_nt(xl, yh) + _dot_nt(xh, yl))


def _rms(x, g):
    return x * lax.rsqrt(jnp.mean(x * x, axis=-1, keepdims=True) + NORM_EPS) * g


def _inproj_body(x_ref, g_ref, w_ref, o_ref, *, norm):
    x = x_ref[...]
    if norm:
        x = _rms(x, g_ref[...])
    o_ref[...] = _dot(x.astype(BF16), w_ref[...])


def _inproj(x, g, w_bf, *, norm, tm):
    rows, ncol = x.shape[0], w_bf.shape[1]
    return pl.pallas_call(
        functools.partial(_inproj_body, norm=norm),
        grid=(rows // tm,),
        in_specs=[pl.BlockSpec((tm, D_MODEL), lambda i: (i, 0)),
                  pl.BlockSpec((1, D_MODEL), lambda i: (0, 0)),
                  pl.BlockSpec((D_MODEL, ncol), lambda i: (0, 0))],
        out_specs=pl.BlockSpec((tm, ncol), lambda i: (i, 0)),
        out_shape=jax.ShapeDtypeStruct((rows, ncol), F32),
        compiler_params=_cparams(1), name="inproj")(x, g, w_bf)


def _rmsnorm_body(x_ref, g_ref, o_ref):
    o_ref[...] = _rms(x_ref[...], g_ref[...])


def _rmsnorm_rows(x, g):
    rows = x.shape[0]
    return pl.pallas_call(
        _rmsnorm_body, grid=(1,),
        in_specs=[pl.BlockSpec((rows, D_MODEL), lambda i: (0, 0)), pl.BlockSpec((1, D_MODEL), lambda i: (0, 0))],
        out_specs=pl.BlockSpec((rows, D_MODEL), lambda i: (0, 0)),
        out_shape=jax.ShapeDtypeStruct((rows, D_MODEL), F32),
        compiler_params=_cparams(1), name="rmsnorm_rows")(x, g)


def _rope(x, c, s_up, s_dn):
    up = pltpu.roll(x, LANES - ROT_DIM // 2, 1)
    dn = pltpu.roll(x, ROT_DIM // 2, 1)
    return x * c + up * s_up + dn * s_dn


def _prep_body(*refs, seq_rows, tile_rows, sample):
    if sample:
        p_ref, prev_ref = refs[0], refs[1]
    else:
        p_ref, prev8_ref = refs[0], refs[1]
    (cos_ref, sup_ref, sdn_ref, mu_ref, w0_ref, a0_ref, kk_ref, ka_ref, rk_ref,
     w2h_ref, w2l_ref, a2h_ref, a2l_ref, g2h_ref, g2l_ref, bd_ref) = refs[2:18]
    r_o, dec_o, k_o, v_o, av_o, bv_o, g_o, bon_o, q_o, ka_o, va_o = refs[18:]

    prw = p_ref[:, :N_RW_COLS]
    rolled = pltpu.roll(prw, 1, 0)
    row = lax.broadcasted_iota(jnp.int32, (tile_rows, 1), 0)
    if sample:
        pprev = jnp.where(row % seq_rows == 0, prev_ref[:, :N_RW_COLS], rolled)
    else:
        tiles_per_seq = seq_rows // tile_rows
        at_start = (pl.program_id(0) % tiles_per_seq) == 0
        first = jnp.where(at_start, 0.0, prev8_ref[SUBLANES - 1:SUBLANES, :N_RW_COLS])
        pprev = jnp.where(row == 0, first, rolled)
    m = prw + (pprev - prw) * mu_ref[...]

    c = RWKV_WIDTH
    xr, xk, xv = m[:, :c], m[:, c:2 * c], m[:, 2 * c:3 * c]
    xwa = m[:, 3 * c:3 * c + LANES]
    xg = m[:, 3 * c + LANES:3 * c + 2 * LANES]

    lw = w0_ref[...] + _dot3(jnp.tanh(xwa), w2h_ref[...], w2l_ref[...])
    z = -lw
    w = -(jnp.maximum(z, 0.0) + jnp.log1p(jnp.exp(-jnp.abs(z)))) - 0.5
    dec_o[...] = jnp.exp(-jnp.exp(w))
    a = jax.nn.sigmoid(a0_ref[...] + _dot3(xwa, a2h_ref[...], a2l_ref[...]))
    g_o[...] = _dot3(jax.nn.sigmoid(xg), g2h_ref[...], g2l_ref[...])

    bd = bd_ref[...]
    kk = xk * kk_ref[...]
    kk = kk / jnp.maximum(jnp.sqrt(_dot_exact01(kk * kk, bd)), 1e-12)
    kmod = xk * (1.0 + (a - 1.0) * ka_ref[...])
    r_o[...] = xr
    k_o[...] = kmod
    v_o[...] = xv
    av_o[...] = -kk
    bv_o[...] = kk * a
    bon_o[...] = _dot_exact01(xr * kmod * rk_ref[...], bd) * xv

    cs, su, sd = cos_ref[...], sup_ref[...], sdn_ref[...]
    o = N_RW_COLS
    for s in range(ATTN_WIDTH // LANES):
        q_o[:, s * LANES:(s + 1) * LANES] = _rope(p_ref[:, o + s * LANES:o + (s + 1) * LANES], cs, su, sd) * ATTN_SCALE
    o += ATTN_WIDTH
    ka_o[...] = _rope(p_ref[:, o:o + KV_WIDTH], cs, su, sd)
    va_o[...] = p_ref[:, o + KV_WIDTH:o + 2 * KV_WIDTH]


def _prep(p, prev, tabs, params, *, seq_rows, tm, sample):
    rows = p.shape[0]
    n_tiles = rows // tm
    row_spec = lambda w: pl.BlockSpec((tm, w), lambda i: (i, 0))
    const = lambda a: pl.BlockSpec(a.shape, lambda i: (0,) * a.ndim)
    if sample:
        prev_spec = row_spec(N_IN_COLS)
        tab_spec = row_spec(LANES)
    else:
        blocks = tm // SUBLANES
        prev_spec = pl.BlockSpec((SUBLANES, N_IN_COLS), lambda i: (jnp.maximum(i * blocks - 1, 0), 0))
        tiles_per_seq = seq_rows // tm
        tab_spec = pl.BlockSpec((tm, LANES), lambda i: (i % tiles_per_seq, 0))
    in_specs = [row_spec(N_IN_COLS), prev_spec, tab_spec, tab_spec, tab_spec] + [const(a) for a in params]
    widths = [RWKV_WIDTH] * 8 + [ATTN_WIDTH, KV_WIDTH, KV_WIDTH]
    return pl.pallas_call(
        functools.partial(_prep_body, seq_rows=seq_rows, tile_rows=tm, sample=sample),
        grid=(n_tiles,), in_specs=in_specs,
        out_specs=[row_spec(w) for w in widths],
        out_shape=[jax.ShapeDtypeStruct((rows, w), F32) for w in widths],
        compiler_params=_cparams(1), name="prep")(p, prev, *tabs, *params)


def _scan_body(r_ref, w_ref, k_ref, v_ref, a_ref, b_ref, s0_ref, bd_ref, eye_ref, y_ref, so_ref, st_s, *row_s, nb, tc):
    ci = pl.program_id(1)
    slabs = RWKV_WIDTH // LANES
    n_tiles = nb * slabs
    n_pairs = n_tiles // 2
    heads_per_tile = LANES // HEAD_DIM

    def tile_states(p):
        return [((p // slabs) * RWKV_HEADS + heads_per_tile * (p % slabs) + m, m * HEAD_DIM) for m in range(heads_per_tile)]

    @pl.when(ci == 0)
    def _():
        for p in range(n_tiles):
            for s, off in tile_states(p):
                st_s[p, :, off:off + HEAD_DIM] = s0_ref[s]

    bd = bd_ref[...]
    eye = eye_ref[...]

    r_s, w_s, k_s, v_s, a_s, b_s, y_s = row_s
    for src, dst in zip((r_ref, w_ref, k_ref, v_ref, a_ref, b_ref), row_s):
        for p in range(n_tiles):
            dst[p] = src[p // slabs, :, (p % slabs) * LANES:(p % slabs + 1) * LANES]

    def row(ref, t, p):
        return ref[p, pl.ds(t, 1), :]

    def paired(tiles):
        return jnp.concatenate([jnp.concatenate([tiles[p], tiles[p + n_pairs]], axis=1) for p in range(n_pairs)], axis=0)

    def unpaired(x):
        rows = lambda p: slice((p % n_pairs) * HEAD_DIM, (p % n_pairs + 1) * HEAD_DIM)
        return [x[rows(p), (p // n_pairs) * LANES:(p // n_pairs + 1) * LANES] for p in range(n_tiles)]

    def seg_sum2(tiles):
        parts = [_split2(x) for x in tiles]
        return unpaired(_dot(paired([p[0] for p in parts]), bd) + _dot(paired([p[1] for p in parts]), bd))

    def step(t, carry):
        st = [st_s[p] for p in range(n_tiles)]
        sa = seg_sum2([st[p] * row(a_s, t, p) for p in range(n_tiles)])
        vc = seg_sum2([eye * row(v_s, t, p) for p in range(n_tiles)])
        p2 = []
        for p in range(n_tiles):
            new = st[p] * row(w_s, t, p) + sa[p] * row(b_s, t, p) + vc[p] * row(k_s, t, p)
            st_s[p] = new
            p2.append((new * row(r_s, t, p)).astype(BF16))
        yc = unpaired(_dot(paired(p2), bd))
        for p in range(n_tiles):
            y_s[p, pl.ds(t, 1), :] = jnp.sum(yc[p] * eye, axis=0, keepdims=True)
        return carry

    lax.fori_loop(0, tc, step, 0)
    for p in range(n_tiles):
        y_ref[p // slabs, :, (p % slabs) * LANES:(p % slabs + 1) * LANES] = y_s[p]

    @pl.when(ci == pl.num_programs(1) - 1)
    def _():
        for p in range(n_tiles):
            for s, off in tile_states(p):
                so_ref[s] = st_s[p, :, off:off + HEAD_DIM]


def _scan(r, w, k, v, a, b, s0, *, nb, tc):
    batch, t_len = r.shape[0], r.shape[1]
    ns = nb * RWKV_HEADS
    bd = _block_ones(2 * LANES, HEAD_DIM)
    eye = jnp.asarray(np.arange(LANES)[None, :] % HEAD_DIM == np.arange(HEAD_DIM)[:, None], dtype=F32)
    seq_spec = pl.BlockSpec((nb, tc, RWKV_WIDTH), lambda g, c: (g, c, 0))
    st_spec = pl.BlockSpec((ns, HEAD_DIM, HEAD_DIM), lambda g, c: (g, 0, 0))
    const = lambda x: pl.BlockSpec(x.shape, lambda g, c: (0,) * x.ndim)
    return pl.pallas_call(
        functools.partial(_scan_body, nb=nb, tc=tc),
        grid=(batch // nb, t_len // tc),
        in_specs=[seq_spec] * 6 + [st_spec, const(bd), const(eye)],
        out_specs=[seq_spec, st_spec],
        out_shape=[jax.ShapeDtypeStruct(r.shape, F32), jax.ShapeDtypeStruct(s0.shape, F32)],
        scratch_shapes=[pltpu.VMEM((nb * RWKV_WIDTH // LANES, HEAD_DIM, LANES), F32)]
        + [pltpu.VMEM((nb * RWKV_WIDTH // LANES, tc, LANES), F32)] * 7,
        compiler_params=_cparams(2), name="wkv_scan")(r, w, k, v, a, b, s0, bd, eye)


def _sink_softmax(sc, valid, sink):
    sm = jnp.where(valid, sc, NEG_INF)
    mx = jnp.maximum(jnp.max(sm, axis=-1, keepdims=True), sink)
    p = jnp.exp(sm - mx)
    return p / (jnp.sum(p, axis=-1, keepdims=True) + jnp.exp(sink - mx))


def _attn_prompt_body(sink_ref, q_ref, kp_ref, kc_ref, vp_ref, vc_ref, o_ref, *, front):
    n = pl.program_id(1)
    keys = jnp.concatenate([kp_ref[...], kc_ref[...]], axis=0)
    vals = jnp.concatenate([vp_ref[...], vc_ref[...]], axis=0)
    shp = (ATT_BLOCK, 2 * ATT_BLOCK)
    qpos = n * ATT_BLOCK - front + lax.broadcasted_iota(jnp.int32, shp, 0)
    kpos = (n - 1) * ATT_BLOCK - front + lax.broadcasted_iota(jnp.int32, shp, 1)
    dlt = qpos - kpos
    valid = (kpos >= 0) & (dlt >= 0) & (dlt < WINDOW)
    half = lax.broadcasted_iota(jnp.int32, (ATT_BLOCK, LANES), 1) // HEAD_DIM
    for s in range(ATTN_WIDTH // LANES):
        q = q_ref[:, s * LANES:(s + 1) * LANES]
        acc = jnp.zeros((ATT_BLOCK, LANES), F32)
        for g in range(KV_HEADS):
            sc = _dot3_nt(jnp.where(half == g, q, 0.0), keys)
            p = _sink_softmax(sc, valid, sink_ref[s + g * (ATTN_HEADS // KV_HEADS)])
            ph, plo = _split2(p)
            vh, vl = _split2(vals)
            o = _dot(ph, vh) + (_dot(plo, vh) + _dot(ph, vl))
            acc = acc + jnp.where(half == g, o, 0.0)
        o_ref[:, s * LANES:(s + 1) * LANES] = acc


def _attn_prompt(sinks, q, k, v, *, front):
    batch, t_len = q.shape[0], q.shape[1]
    blk = lambda w, f: pl.BlockSpec((None, ATT_BLOCK, w), f)
    cur = lambda b, n: (b, n, 0)
    prv = lambda b, n: (b, jnp.maximum(n - 1, 0), 0)
    return pl.pallas_call(
        functools.partial(_attn_prompt_body, front=front),
        grid=(batch, t_len // ATT_BLOCK),
        in_specs=[pl.BlockSpec(memory_space=pltpu.SMEM), blk(ATTN_WIDTH, cur),
                  blk(KV_WIDTH, prv), blk(KV_WIDTH, cur), blk(KV_WIDTH, prv), blk(KV_WIDTH, cur)],
        out_specs=blk(ATTN_WIDTH, cur),
        out_shape=jax.ShapeDtypeStruct(q.shape, F32),
        compiler_params=_cparams(2), name="attn_prompt")(sinks, q, k, k, v, v)


def _attn_sample_body(sink_ref, q_ref, kn_ref, vn_ref, kb_ref, vb_ref, o_ref, *, bb, n_new):
    keys = jnp.concatenate([kb_ref[...], kn_ref[...]], axis=1)
    vals = jnp.concatenate([vb_ref[...], vn_ref[...]], axis=1)
    n_keys = WINDOW + SUBLANES
    shp = (n_new, n_keys)
    tq = lax.broadcasted_iota(jnp.int32, shp, 0)
    sk = lax.broadcasted_iota(jnp.int32, shp, 1)
    dlt = tq - (sk - WINDOW)
    valid = ((dlt >= 0) & (dlt < WINDOW))[None]
    half = lax.broadcasted_iota(jnp.int32, (bb, n_new, LANES), 2) // HEAD_DIM
    bdot = lambda x, y, cd: lax.dot_general(x, y, ((cd, ((0,), (0,)))), preferred_element_type=F32)
    kh, kl = _split2(keys)
    vh, vl = _split2(vals)
    for s in range(ATTN_WIDTH // LANES):
        q = q_ref[:, :, s * LANES:(s + 1) * LANES]
        acc = jnp.zeros((bb, n_new, LANES), F32)
        for g in range(KV_HEADS):
            qh, ql = _split2(jnp.where(half == g, q, 0.0))
            nt = ((2,), (2,))
            sc = bdot(qh, kh, nt) + (bdot(ql, kh, nt) + bdot(qh, kl, nt))
            p = _sink_softmax(sc, valid, sink_ref[s + g * (ATTN_HEADS // KV_HEADS)])
            ph, plo = _split2(p)
            nn = ((2,), (1,))
            o = bdot(ph, vh, nn) + (bdot(plo, vh, nn) + bdot(ph, vl, nn))
            acc = acc + jnp.where(half == g, o, 0.0)
        o_ref[:, :, s * LANES:(s + 1) * LANES] = acc


def _attn_sample(sinks, q, k_new, v_new, k_buf, v_buf, *, bb):
    batch, n_new = q.shape[0], q.shape[1]
    spec = lambda a: pl.BlockSpec((bb,) + a.shape[1:], lambda i: (i, 0, 0))
    return pl.pallas_call(
        functools.partial(_attn_sample_body, bb=bb, n_new=n_new),
        grid=(batch // bb,),
        in_specs=[pl.BlockSpec(memory_space=pltpu.SMEM)] + [spec(a) for a in (q, k_new, v_new, k_buf, v_buf)],
        out_specs=spec(q),
        out_shape=jax.ShapeDtypeStruct(q.shape, F32),
        compiler_params=_cparams(1), name="attn_sample")(sinks, q, k_new, v_new, k_buf, v_buf)


def _mix_body(x_ref, y_ref, g_ref, bon_ref, ya_ref, lw_ref, lb_ref, bd_ref, wo_ref, n2_ref, wq_ref,
              skh_ref, skl_ref, x1_o, h2_o, sc_o):
    bd = bd_ref[...]
    y = y_ref[...]
    inv = 1.0 / HEAD_DIM
    mean = _dot_exact01(y, bd) * inv
    yc = y - mean
    var = _dot_exact01(yc * yc, bd) * inv
    yn = yc * lax.rsqrt(var + LNX_EPS) * lw_ref[...] + lb_ref[...]
    yrw = (yn + bon_ref[...]) * g_ref[...]
    x1 = x_ref[...] + (_dot(yrw.astype(BF16), wo_ref[:RWKV_WIDTH, :]) + _dot(ya_ref[...].astype(BF16), wo_ref[RWKV_WIDTH:, :]))
    x1_o[...] = x1
    h2 = _rms(x1, n2_ref[...]).astype(BF16)
    h2_o[...] = h2
    qp = _dot(h2, wq_ref[...])
    for hc in range(2 * PEER_HEADS):
        c = hc % 2
        qh, ql = _split2(qp[:, hc * D_HALF:(hc + 1) * D_HALF])
        sc_o[hc] = _dot_nt(skh_ref[c], qh) + (_dot_nt(skh_ref[c], ql) + _dot_nt(skl_ref[c], qh))


def _mix(x, y, g, bon, ya, params, *, tm, skip):
    batch, t_len = x.shape[0], x.shape[1]
    n_out = t_len // tm - skip
    rows = batch * n_out * tm
    src = lambda w: pl.BlockSpec((None, tm, w), lambda b, j: (b, j + skip, 0))
    dst = lambda w: pl.BlockSpec((tm, w), lambda b, j: (b * n_out + j, 0))
    const = lambda a: pl.BlockSpec(a.shape, lambda b, j: (0,) * a.ndim)
    return pl.pallas_call(
        _mix_body, grid=(batch, n_out),
        in_specs=[src(D_MODEL), src(RWKV_WIDTH), src(RWKV_WIDTH), src(RWKV_WIDTH), src(ATTN_WIDTH)]
        + [const(a) for a in params],
        out_specs=[dst(D_MODEL), dst(D_MODEL),
                   pl.BlockSpec((2 * PEER_HEADS, N_KEYS, tm), lambda b, j: (0, 0, b * n_out + j))],
        out_shape=[jax.ShapeDtypeStruct((rows, D_MODEL), F32),
                   jax.ShapeDtypeStruct((rows, D_MODEL), BF16),
                   jax.ShapeDtypeStruct((2 * PEER_HEADS, N_KEYS, rows), F32)],
        compiler_params=_cparams(2), name="mix")(x, y, g, bon, ya, *params)


def _pair_tables(tt):
    pairs = [(k1, k2) for k1 in range(PEER_TOPK) for k2 in range(PEER_TOPK) if (k1 + 1) * (k2 + 1) <= PEER_TOPK]
    n_rows = -(-len(pairs) // SUBLANES) * SUBLANES
    g1 = np.zeros((n_rows, N_KEYS), np.float32)
    g2 = np.zeros((n_rows, N_KEYS), np.float32)
    neg = np.full((n_rows, tt), -np.inf, np.float32)
    flat = np.full((n_rows, tt), float(PEER_TOPK * PEER_TOPK), np.float32)
    for r, (k1, k2) in enumerate(pairs):
        g1[r, k1] = 1.0
        g2[r, k2] = 1.0
        neg[r] = 0.0
        flat[r] = k1 * PEER_TOPK + k2
    return jnp.asarray(g1, dtype=BF16), jnp.asarray(g2, dtype=BF16), jnp.asarray(neg), jnp.asarray(flat)


def _topk_body(s_ref, g1_ref, g2_ref, neg_ref, flat_ref, ia_o, ib_o, gt_o, v_s, i_s, a_s, b_s, t_s, *, tt):
    key = lax.broadcasted_iota(jnp.int32, (N_KEYS, tt), 0).astype(F32)
    rank = lax.broadcasted_iota(jnp.int32, (PEER_TOPK, tt), 0).astype(F32)
    neg, flat = neg_ref[...], flat_ref[...]
    v_s[...] = jnp.zeros((2, N_KEYS, tt), F32)

    def expand(g, v):
        hi, mid, lo = _split3(v)
        return _dot(g, hi) + _dot(g, mid) + _dot(g, lo)

    def head(h, carry):
        for c in range(2):
            x = s_ref[2 * h + c]
            for it in range(PEER_TOPK):
                m = jnp.max(x, axis=0, keepdims=True)
                am = jnp.min(jnp.where(x == m, key, float(N_KEYS)), axis=0, keepdims=True)
                v_s[c, it:it + 1, :] = m
                i_s[c, it:it + 1, :] = am
                x = jnp.where(key == am, -jnp.inf, x)
        cand = expand(g1_ref[...], v_s[0]) + expand(g2_ref[...], v_s[1]) + neg
        i1, i2 = i_s[0], i_s[1]
        base = pl.multiple_of(h * PEER_TOPK, PEER_TOPK)
        for it in range(PEER_TOPK):
            m = jnp.max(cand, axis=0, keepdims=True)
            fl = jnp.min(jnp.where(cand == m, flat, float(PEER_TOPK * PEER_TOPK)), axis=0, keepdims=True)
            k1 = jnp.floor(fl * (1.0 / PEER_TOPK))
            k2 = fl - k1 * PEER_TOPK
            a_s[pl.ds(base + it, 1), :] = jnp.sum(jnp.where(rank == k1, i1, 0.0), axis=0, keepdims=True)
            b_s[pl.ds(base + it, 1), :] = jnp.sum(jnp.where(rank == k2, i2, 0.0), axis=0, keepdims=True)
            t_s[pl.ds(base + it, 1), :] = m
            cand = jnp.where(flat == fl, -jnp.inf, cand)
        top = t_s[pl.ds(base, PEER_TOPK), :]
        e = jnp.exp(top - top[0:1])
        t_s[pl.ds(base, PEER_TOPK), :] = e / jnp.sum(e, axis=0, keepdims=True)
        return carry

    lax.fori_loop(0, PEER_HEADS, head, 0)
    ia_o[...] = a_s[...].T.astype(jnp.int32)
    ib_o[...] = b_s[...].T.astype(jnp.int32)
    gt_o[...] = t_s[...].T


def _topk(scores_t, *, tt):
    rows = scores_t.shape[2]
    tabs = _pair_tables(tt)
    out = pl.BlockSpec((tt, N_SEL), lambda i: (i, 0))
    const = lambda a: pl.BlockSpec(a.shape, lambda i: (0,) * a.ndim)
    sel_scr = pltpu.VMEM((N_SEL, tt), F32)
    return pl.pallas_call(
        functools.partial(_topk_body, tt=tt), grid=(rows // tt,),
        in_specs=[pl.BlockSpec((2 * PEER_HEADS, N_KEYS, tt), lambda i: (0, 0, i))] + [const(a) for a in tabs],
        out_specs=[out, out, out],
        out_shape=[jax.ShapeDtypeStruct((rows, N_SEL), jnp.int32), jax.ShapeDtypeStruct((rows, N_SEL), jnp.int32),
                   jax.ShapeDtypeStruct((rows, N_SEL), F32)],
        scratch_shapes=[pltpu.VMEM((2, N_KEYS, tt), F32), pltpu.VMEM((2, PEER_TOPK, tt), F32), sel_scr, sel_scr, sel_scr],
        compiler_params=_cparams(1), name="peer_topk")(scores_t, *tabs)


def _peer_body(h_ref, ia_ref, ib_ref, gt_ref, x1_ref, fg_ref, eu_ref, ev_ref, o_ref, gate_s, acc_s, *, tm, n_a):
    e = pl.program_id(1)

    @pl.when(e == 0)
    def _():
        acc_s[...] = jnp.zeros((tm, D_MODEL), F32)
        sub = lax.broadcasted_iota(jnp.int32, (N_KEYS, N_SEL), 0)

        def build(t, c):
            pa = jnp.where(sub == ia_ref[pl.ds(t, 1), :], gt_ref[pl.ds(t, 1), :], 0.0).astype(BF16)
            qb = jnp.where(sub == ib_ref[pl.ds(t, 1), :], 1.0, 0.0).astype(BF16)
            gate_s[pl.ds(pl.multiple_of(t * GATE_PITCH, SUBLANES), N_KEYS), :] = _dot_nt(pa, qb)
            return c

        lax.fori_loop(0, tm, build, 0, unroll=GATE_BUILD_UNROLL)

    pre = _dot(h_ref[...], eu_ref[...])
    acts = []
    for ai in range(n_a):
        x = pre[:, ai * N_KEYS:(ai + 1) * N_KEYS]
        gate = gate_s[pl.ds(e * n_a + ai, tm, stride=GATE_PITCH), :]
        acts.append((0.5 * x * (1.0 + lax.erf(x * np.float32(np.sqrt(0.5)))) * gate).astype(BF16))
    acc_s[...] += _dot(jnp.concatenate(acts, axis=1), ev_ref[...])

    @pl.when(e == pl.num_programs(1) - 1)
    def _():
        o_ref[...] = _rms(x1_ref[...] + acc_s[...], fg_ref[...])


def _peer(h2, ia, ib, gt, x1, fg, eu_t, ev, *, tm, n_a):
    rows = h2.shape[0]
    eb = n_a * N_KEYS
    tok = lambda w: pl.BlockSpec((tm, w), lambda i, e: (i, 0))
    return pl.pallas_call(
        functools.partial(_peer_body, tm=tm, n_a=n_a),
        grid=(rows // tm, N_EXPERTS // eb),
        in_specs=[tok(D_MODEL), tok(N_SEL), tok(N_SEL), tok(N_SEL), tok(D_MODEL),
                  pl.BlockSpec((1, D_MODEL), lambda i, e: (0, 0)),
                  pl.BlockSpec((D_MODEL, eb), lambda i, e: (0, e)),
                  pl.BlockSpec((eb, D_MODEL), lambda i, e: (e, 0))],
        out_specs=tok(D_MODEL),
        out_shape=jax.ShapeDtypeStruct((rows, D_MODEL), F32),
        scratch_shapes=[pltpu.VMEM((tm * GATE_PITCH, N_KEYS), F32), pltpu.VMEM((tm, D_MODEL), F32)],
        compiler_params=_cparams(2), name="peer_experts")(h2, ia, ib, gt, x1, fg, eu_t, ev)


def _rope_tables(pos):
    half = ROT_DIM // 2
    inv_freq = ROPE_THETA ** (-jnp.arange(0, ROT_DIM, 2, dtype=F32) / ROT_DIM)
    ang = pos.astype(F32)[:, None] * inv_freq[None, :]
    cos, sin = jnp.cos(ang), jnp.sin(ang)
    n = pos.shape[0]
    ones = jnp.ones((n, HEAD_DIM - ROT_DIM), F32)
    zeros = jnp.zeros((n, HEAD_DIM - ROT_DIM), F32)
    zh = jnp.zeros((n, half), F32)
    c = jnp.concatenate([cos, cos, ones], axis=1)
    s_up = jnp.concatenate([-sin, zh, zeros], axis=1)
    s_dn = jnp.concatenate([zh, sin, zeros], axis=1)
    rep = LANES // HEAD_DIM
    return tuple(jnp.tile(t, (1, rep)) for t in (c, s_up, s_dn))


def _block_ones(n, blk):
    i = np.arange(n) // blk
    return jnp.asarray((i[:, None] == i[None, :]).astype(np.float32), dtype=BF16)


def _hi_lo(w):
    hi = w.astype(BF16)
    return hi, (w - hi.astype(F32)).astype(BF16)


def kernel(x_prompt, x_sample, cache_k_win, cache_v_win, state_wkv, state_shift, meta_tokens, norm1_g, w_in, mu_shift, w0, w_lora_w2, a0, w_lora_a2, w_lora_g2, k_k, k_a, r_k, lnx_w, lnx_b, attn_sinks, w_out, norm2_g, w_query, sub_keys, expert_u, expert_v, final_norm_g):
    batch, seq = x_prompt.shape[0], x_prompt.shape[1]
    dec_batch, dec_seq = x_sample.shape[0], x_sample.shape[1]
    l_real = N_META + seq
    front = (-l_real) % ATT_BLOCK
    l_pad = l_real + front
    head_rows = front + N_META
    assert head_rows % ATT_BLOCK == 0 and dec_seq <= SUBLANES and cache_k_win.shape[2] == WINDOW

    hpg = ATTN_HEADS // KV_HEADS
    qperm = np.concatenate([(s + hpg * g) * HEAD_DIM + np.arange(HEAD_DIM) for s in range(hpg) for g in range(KV_HEADS)])
    win = w_in[0]
    win = jnp.concatenate([win[:, :N_RW_COLS], win[:, N_RW_COLS + qperm], win[:, N_RW_COLS + ATTN_WIDTH:]], axis=1).astype(BF16)
    wout = w_out[0]
    wout = jnp.concatenate([wout[:RWKV_WIDTH], wout[RWKV_WIDTH + qperm]], axis=0).astype(BF16)
    sinks = attn_sinks[0]

    row = lambda a: a.reshape(1, -1)
    zpad = lambda w, top: jnp.concatenate([w, jnp.zeros_like(w)] if top else [jnp.zeros_like(w), w], axis=0)
    bd64 = _block_ones(RWKV_WIDTH, HEAD_DIM)
    prep_params = (row(mu_shift[0]), row(w0[0]), row(a0[0]), row(k_k[0]), row(k_a[0]), row(r_k[0]),
                   *_hi_lo(zpad(w_lora_w2[0], True)), *_hi_lo(zpad(w_lora_a2[0], False)), *_hi_lo(w_lora_g2[0]), bd64)
    mix_params = (row(lnx_w[0]), row(lnx_b[0]), bd64, wout, row(norm2_g[0]), w_query[0].astype(BF16), *_hi_lo(sub_keys[0]))
    eu_t = expert_u[0].astype(BF16).T
    ev = expert_v[0].astype(BF16)
    g1, fg = row(norm1_g[0]), row(final_norm_g)

    def ffn(x1, h2, scores_t):
        ia, ib, gt = _topk(scores_t, tt=LANES)
        return _peer(h2, ia, ib, gt, x1, fg, eu_t, ev, tm=min(PEER_TOKEN_TILE, x1.shape[0]), n_a=4)

    xp = jnp.concatenate([jnp.zeros((batch, front, D_MODEL), F32),
                          jnp.broadcast_to(meta_tokens[None], (batch, N_META, D_MODEL)), x_prompt], axis=1)
    proj = _inproj(xp.reshape(batch * l_pad, D_MODEL), g1, win, norm=True, tm=256)
    tabs = _rope_tables(jnp.arange(l_pad) - front)
    r, dec, kx, vx, av, bv, gg, bon, q, ka, va = _prep(proj, proj, tabs, prep_params, seq_rows=l_pad, tm=384, sample=False)
    b3 = lambda a: a.reshape(batch, l_pad, -1)
    y_rw, s_p = _scan(b3(r), b3(dec), b3(kx), b3(vx), b3(av), b3(bv),
                      jnp.zeros((batch * RWKV_HEADS, HEAD_DIM, HEAD_DIM), F32), nb=batch, tc=64)
    y_at = _attn_prompt(sinks, b3(q), b3(ka), b3(va), front=front)
    x1, h2, scores = _mix(xp, y_rw, b3(gg), b3(bon), y_at, mix_params, tm=ATT_BLOCK, skip=head_rows // ATT_BLOCK)
    y_prompt = ffn(x1, h2, scores).reshape(batch, seq, D_MODEL)
    keep = min(WINDOW, l_real)
    k_win_p = b3(ka)[:, l_pad - keep:].reshape(1, batch, keep, KV_HEADS, HEAD_DIM)
    v_win_p = b3(va)[:, l_pad - keep:].reshape(1, batch, keep, KV_HEADS, HEAD_DIM)
    wkv_p = s_p.reshape(1, batch, RWKV_HEADS, HEAD_DIM, HEAD_DIM)

    n_s = dec_batch * dec_seq
    xs = x_sample.reshape(n_s, D_MODEL)
    proj_s = _inproj(xs, g1, win, norm=True, tm=256)
    prev_proj = _inproj(state_shift[0], g1, win, norm=False, tm=dec_batch)
    prev_rows = jnp.pad(prev_proj[:, None, :], ((0, 0), (0, dec_seq - 1), (0, 0))).reshape(n_s, N_IN_COLS)
    pos_s = jnp.tile(PAST_LEN + jnp.arange(dec_seq), dec_batch)
    r, dec, kx, vx, av, bv, gg, bon, q, ka, va = _prep(proj_s, prev_rows, _rope_tables(pos_s), prep_params,
                                                       seq_rows=dec_seq, tm=n_s, sample=True)
    s3 = lambda a: a.reshape(dec_batch, dec_seq, -1)
    y_rw, s_s = _scan(s3(r), s3(dec), s3(kx), s3(vx), s3(av), s3(bv),
                      state_wkv[0].reshape(dec_batch * RWKV_HEADS, HEAD_DIM, HEAD_DIM), nb=4, tc=dec_seq)
    k_buf = cache_k_win[0].reshape(dec_batch, WINDOW, KV_WIDTH)
    v_buf = cache_v_win[0].reshape(dec_batch, WINDOW, KV_WIDTH)
    pad8 = lambda a: jnp.pad(a, ((0, 0), (0, SUBLANES - dec_seq), (0, 0)))
    y_at = _attn_sample(sinks, s3(q), pad8(s3(ka)), pad8(s3(va)), k_buf, v_buf, bb=8)
    one = lambda a: a.reshape(1, n_s, -1)
    x1, h2, scores = _mix(one(xs), one(y_rw), one(gg), one(bon), one(y_at), mix_params, tm=ATT_BLOCK, skip=0)
    y_sample = ffn(x1, h2, scores).reshape(dec_batch, dec_seq, D_MODEL)
    k_win_s = jnp.concatenate([k_buf, s3(ka)], axis=1)[:, -WINDOW:].reshape(1, dec_batch, WINDOW, KV_HEADS, HEAD_DIM)
    v_win_s = jnp.concatenate([v_buf, s3(va)], axis=1)[:, -WINDOW:].reshape(1, dec_batch, WINDOW, KV_HEADS, HEAD_DIM)
    wkv_s = s_s.reshape(1, dec_batch, RWKV_HEADS, HEAD_DIM, HEAD_DIM)

    last = jnp.concatenate([x_prompt[:, -1], x_sample[:, -1]], axis=0)
    pad_rows = (-last.shape[0]) % SUBLANES
    shift = _rmsnorm_rows(jnp.pad(last, ((0, pad_rows), (0, 0))), g1)
    shift_p = shift[:batch].reshape(1, batch, D_MODEL)
    shift_s = shift[batch:batch + dec_batch].reshape(1, dec_batch, D_MODEL)

    return (y_prompt, y_sample, k_win_p, v_win_p, wkv_p, shift_p, k_win_s, v_win_s, wkv_s, shift_s)
```

```python
import functools

import numpy as np
import jax
import jax.numpy as jnp
from jax import lax
from jax.experimental import pallas as pl
from jax.experimental.pallas import tpu as pltpu

F32 = jnp.float32
BF16 = jnp.bfloat16

D_MODEL = 1024
HEAD_DIM = 64
RWKV_WIDTH = 512
RWKV_HEADS = 8
ATTN_WIDTH = 512
ATTN_HEADS = 8
KV_HEADS = 2
KV_WIDTH = KV_HEADS * HEAD_DIM
DECAY_LORA = 64
AAA_LORA = 64
GATE_LORA = 128
N_RW_COLS = 3 * RWKV_WIDTH + DECAY_LORA + AAA_LORA + GATE_LORA
N_IN_COLS = N_RW_COLS + ATTN_WIDTH + 2 * KV_WIDTH
LNX_EPS = 64e-5
WINDOW = 128
ATT_BLOCK = 128
ATTN_SCALE = HEAD_DIM ** -0.5
ROPE_THETA = 500000.0
ROT_DIM = HEAD_DIM // 4
N_META = 16
N_KEYS = 128
N_EXPERTS = N_KEYS * N_KEYS
PEER_HEADS = 8
PEER_TOPK = 16
N_SEL = PEER_HEADS * PEER_TOPK
D_HALF = 128
NORM_EPS = 1e-5
NEG_INF = -1e30
PAST_LEN = 8192

LANES = 128
SUBLANES = 8
GATE_ROWS = 128
GATE_PITCH = GATE_ROWS + SUBLANES
VMEM_LIMIT = 56 * 1024 * 1024
GATE_BUILD_UNROLL = 16
PEER_TOKEN_TILE = 256


def _cparams(n_axes):
    return pltpu.CompilerParams(dimension_semantics=("arbitrary",) * n_axes, vmem_limit_bytes=VMEM_LIMIT)


def _split2(x):
    hi = x.astype(BF16)
    lo = (x - hi.astype(F32)).astype(BF16)
    return hi, lo


def _split3(x):
    hi = x.astype(BF16)
    r1 = x - hi.astype(F32)
    mid = r1.astype(BF16)
    lo = (r1 - mid.astype(F32)).astype(BF16)
    return hi, mid, lo


def _dot(a, b):
    return jnp.dot(a, b, preferred_element_type=F32)


def _dot_nt(a, b):
    return lax.dot_general(a, b, (((1,), (1,)), ((), ())), preferred_element_type=F32)


def _dot_exact01(x, m01):
    hi, mid, lo = _split3(x)
    return _dot(hi, m01) + _dot(mid, m01) + _dot(lo, m01)


def _dot3(x, wh, wl):
    xh, xl = _split2(x)
    return _dot(xh, wh) + (_dot(xl, wh) + _dot(xh, wl))


def _dot3_nt(x, y):
    xh, xl = _split2(x)
    yh, yl = _split2(y)
    return _dot_nt(xh, yh) + (_dot_nt(xl, yh) + _dot_nt(xh, yl))


def _rms(x, g):
    return x * lax.rsqrt(jnp.mean(x * x, axis=-1, keepdims=True) + NORM_EPS) * g


def _inproj_body(x_ref, g_ref, w_ref, o_ref, *, norm):
    x = x_ref[...]
    if norm:
        x = _rms(x, g_ref[...])
    o_ref[...] = _dot(x.astype(BF16), w_ref[...])


def _inproj(x, g, w_bf, *, norm, tm):
    rows, ncol = x.shape[0], w_bf.shape[1]
    return pl.pallas_call(
        functools.partial(_inproj_body, norm=norm),
        grid=(rows // tm,),
        in_specs=[pl.BlockSpec((tm, D_MODEL), lambda i: (i, 0)),
                  pl.BlockSpec((1, D_MODEL), lambda i: (0, 0)),
                  pl.BlockSpec((D_MODEL, ncol), lambda i: (0, 0))],
        out_specs=pl.BlockSpec((tm, ncol), lambda i: (i, 0)),
        out_shape=jax.ShapeDtypeStruct((rows, ncol), F32),
        compiler_params=_cparams(1), name="inproj")(x, g, w_bf)


def _rmsnorm_body(x_ref, g_ref, o_ref):
    o_ref[...] = _rms(x_ref[...], g_ref[...])


def _rmsnorm_rows(x, g):
    rows = x.shape[0]
    return pl.pallas_call(
        _rmsnorm_body, grid=(1,),
        in_specs=[pl.BlockSpec((rows, D_MODEL), lambda i: (0, 0)), pl.BlockSpec((1, D_MODEL), lambda i: (0, 0))],
        out_specs=pl.BlockSpec((rows, D_MODEL), lambda i: (0, 0)),
        out_shape=jax.ShapeDtypeStruct((rows, D_MODEL), F32),
        compiler_params=_cparams(1), name="rmsnorm_rows")(x, g)


def _rope(x, c, s_up, s_dn):
    up = pltpu.roll(x, LANES - ROT_DIM // 2, 1)
    dn = pltpu.roll(x, ROT_DIM // 2, 1)
    return x * c + up * s_up + dn * s_dn


def _prep_body(*refs, seq_rows, tile_rows, sample):
    if sample:
        p_ref, prev_ref = refs[0], refs[1]
    else:
        p_ref, prev8_ref = refs[0], refs[1]
    (cos_ref, sup_ref, sdn_ref, mu_ref, w0_ref, a0_ref, kk_ref, ka_ref, rk_ref,
     w2h_ref, w2l_ref, a2h_ref, a2l_ref, g2h_ref, g2l_ref, bd_ref) = refs[2:18]
    r_o, dec_o, k_o, v_o, av_o, bv_o, g_o, bon_o, q_o, ka_o, va_o = refs[18:]

    prw = p_ref[:, :N_RW_COLS]
    rolled = pltpu.roll(prw, 1, 0)
    row = lax.broadcasted_iota(jnp.int32, (tile_rows, 1), 0)
    if sample:
        pprev = jnp.where(row % seq_rows == 0, prev_ref[:, :N_RW_COLS], rolled)
    else:
        tiles_per_seq = seq_rows // tile_rows
        at_start = (pl.program_id(0) % tiles_per_seq) == 0
        first = jnp.where(at_start, 0.0, prev8_ref[SUBLANES - 1:SUBLANES, :N_RW_COLS])
        pprev = jnp.where(row == 0, first, rolled)
    m = prw + (pprev - prw) * mu_ref[...]

    c = RWKV_WIDTH
    xr, xk, xv = m[:, :c], m[:, c:2 * c], m[:, 2 * c:3 * c]
    xwa = m[:, 3 * c:3 * c + LANES]
    xg = m[:, 3 * c + LANES:3 * c + 2 * LANES]

    lw = w0_ref[...] + _dot3(jnp.tanh(xwa), w2h_ref[...], w2l_ref[...])
    z = -lw
    w = -(jnp.maximum(z, 0.0) + jnp.log1p(jnp.exp(-jnp.abs(z)))) - 0.5
    dec_o[...] = jnp.exp(-jnp.exp(w))
    a = jax.nn.sigmoid(a0_ref[...] + _dot3(xwa, a2h_ref[...], a2l_ref[...]))
    g_o[...] = _dot3(jax.nn.sigmoid(xg), g2h_ref[...], g2l_ref[...])

    bd = bd_ref[...]
    kk = xk * kk_ref[...]
    kk = kk / jnp.maximum(jnp.sqrt(_dot_exact01(kk * kk, bd)), 1e-12)
    kmod = xk * (1.0 + (a - 1.0) * ka_ref[...])
    r_o[...] = xr
    k_o[...] = kmod
    v_o[...] = xv
    av_o[...] = -kk
    bv_o[...] = kk * a
    bon_o[...] = _dot_exact01(xr * kmod * rk_ref[...], bd) * xv

    cs, su, sd = cos_ref[...], sup_ref[...], sdn_ref[...]
    o = N_RW_COLS
    for s in range(ATTN_WIDTH // LANES):
        q_o[:, s * LANES:(s + 1) * LANES] = _rope(p_ref[:, o + s * LANES:o + (s + 1) * LANES], cs, su, sd) * ATTN_SCALE
    o += ATTN_WIDTH
    ka_o[...] = _rope(p_ref[:, o:o + KV_WIDTH], cs, su, sd)
    va_o[...] = p_ref[:, o + KV_WIDTH:o + 2 * KV_WIDTH]


def _prep(p, prev, tabs, params, *, seq_rows, tm, sample):
    rows = p.shape[0]
    n_tiles = rows // tm
    row_spec = lambda w: pl.BlockSpec((tm, w), lambda i: (i, 0))
    const = lambda a: pl.BlockSpec(a.shape, lambda i: (0,) * a.ndim)
    if sample:
        prev_spec = row_spec(N_IN_COLS)
        tab_spec = row_spec(LANES)
    else:
        blocks = tm // SUBLANES
        prev_spec = pl.BlockSpec((SUBLANES, N_IN_COLS), lambda i: (jnp.maximum(i * blocks - 1, 0), 0))
        tiles_per_seq = seq_rows // tm
        tab_spec = pl.BlockSpec((tm, LANES), lambda i: (i % tiles_per_seq, 0))
    in_specs = [row_spec(N_IN_COLS), prev_spec, tab_spec, tab_spec, tab_spec] + [const(a) for a in params]
    widths = [RWKV_WIDTH] * 8 + [ATTN_WIDTH, KV_WIDTH, KV_WIDTH]
    return pl.pallas_call(
        functools.partial(_prep_body, seq_rows=seq_rows, tile_rows=tm, sample=sample),
        grid=(n_tiles,), in_specs=in_specs,
        out_specs=[row_spec(w) for w in widths],
        out_shape=[jax.ShapeDtypeStruct((rows, w), F32) for w in widths],
        compiler_params=_cparams(1), name="prep")(p, prev, *tabs, *params)


def _scan_body(r_ref, w_ref, k_ref, v_ref, a_ref, b_ref, s0_ref, bd_ref, eye_ref, esh_ref, y_ref, so_ref,
               st_s, *row_s, nb, tc):
    ci = pl.program_id(1)
    slabs = RWKV_WIDTH // LANES
    n_tiles = nb * slabs
    n_pairs = n_tiles // 2
    heads_per_tile = LANES // HEAD_DIM
    r_s, w_s, k_s, vh_s, a_s, b_s, vl_s, y_s, p2_s = row_s

    def tile_states(p):
        return [((p // slabs) * RWKV_HEADS + heads_per_tile * (p % slabs) + m, m * HEAD_DIM) for m in range(heads_per_tile)]

    @pl.when(ci == 0)
    def _():
        for p in range(n_tiles):
            for s, off in tile_states(p):
                st_s[p, :, off:off + HEAD_DIM] = s0_ref[s]

        p2_s[...] = jnp.zeros(p2_s.shape, BF16)

    bd = bd_ref[...]
    eye = eye_ref[...]
    eye_far = esh_ref[...]

    half_head = HEAD_DIM // 2
    first_half = lax.broadcasted_iota(jnp.int32, (tc, LANES), 1) % HEAD_DIM < half_head
    for p in range(n_tiles):
        sl = (p // slabs, slice(None), slice((p % slabs) * LANES, (p % slabs + 1) * LANES))
        for src, dst in ((r_ref, r_s), (w_ref, w_s), (k_ref, k_s), (a_ref, a_s), (b_ref, b_s)):
            dst[p] = src[sl]
        v = v_ref[sl]
        hi = v.astype(BF16).astype(F32)
        lo = (v - hi).astype(BF16).astype(F32)
        vh_s[p] = hi
        vl_s[p] = jnp.where(first_half, pltpu.roll(lo, LANES - half_head, 1), pltpu.roll(lo, half_head, 1))

    def row(ref, t, p):
        return ref[p, pl.ds(t, 1), :]

    def paired(tiles):
        return jnp.concatenate([jnp.concatenate([tiles[p], tiles[p + n_pairs]], axis=1) for p in range(n_pairs)], axis=0)

    def unpaired(x):
        rows = lambda p: slice((p % n_pairs) * HEAD_DIM, (p % n_pairs + 1) * HEAD_DIM)
        return [x[rows(p), (p // n_pairs) * LANES:(p // n_pairs + 1) * LANES] for p in range(n_tiles)]

    def seg_sum2(tiles):
        parts = [_split2(x) for x in tiles]
        return unpaired(_dot(paired([p[0] for p in parts]), bd) + _dot(paired([p[1] for p in parts]), bd))

    def emit_y(row_idx):
        yc = unpaired(_dot(p2_s[...], bd))
        for p in range(n_tiles):
            y_s[p, pl.ds(row_idx, 1), :] = jnp.sum(yc[p] * eye, axis=0, keepdims=True)

    def step(t, carry):
        emit_y(t + (SUBLANES - 1))
        st = [st_s[p] for p in range(n_tiles)]
        sa = seg_sum2([st[p] * row(a_s, t, p) for p in range(n_tiles)])
        vd = [(eye * row(vh_s, t, p) + eye_far * row(vl_s, t, p)).astype(BF16) for p in range(n_tiles)]
        vc = unpaired(_dot(paired(vd), bd))
        p2 = []
        for p in range(n_tiles):
            new = st[p] * row(w_s, t, p) + sa[p] * row(b_s, t, p) + vc[p] * row(k_s, t, p)
            st_s[p] = new
            p2.append((new * row(r_s, t, p)).astype(BF16))
        p2_s[...] = paired(p2)
        return carry

    lax.fori_loop(0, tc, step, 0)
    emit_y(tc + (SUBLANES - 1))
    for p in range(n_tiles):
        y_ref[p // slabs, :, (p % slabs) * LANES:(p % slabs + 1) * LANES] = y_s[p, SUBLANES:, :]

    @pl.when(ci == pl.num_programs(1) - 1)
    def _():
        for p in range(n_tiles):
            for s, off in tile_states(p):
                so_ref[s] = st_s[p, :, off:off + HEAD_DIM]


def _scan(r, w, k, v, a, b, s0, *, nb, tc):
    batch, t_len = r.shape[0], r.shape[1]
    ns = nb * RWKV_HEADS
    n_tiles = nb * RWKV_WIDTH // LANES
    bd = _block_ones(2 * LANES, HEAD_DIM)
    lane_in_head = np.arange(LANES)[None, :] % HEAD_DIM
    value_row = np.arange(HEAD_DIM)[:, None]
    eye = jnp.asarray(lane_in_head == value_row, dtype=F32)
    eye_far = jnp.asarray(lane_in_head == (value_row + HEAD_DIM // 2) % HEAD_DIM, dtype=F32)
    seq_spec = pl.BlockSpec((nb, tc, RWKV_WIDTH), lambda g, c: (g, c, 0))
    st_spec = pl.BlockSpec((ns, HEAD_DIM, HEAD_DIM), lambda g, c: (g, 0, 0))
    const = lambda x: pl.BlockSpec(x.shape, lambda g, c: (0,) * x.ndim)
    chunk_scr = pltpu.VMEM((n_tiles, tc, LANES), F32)
    return pl.pallas_call(
        functools.partial(_scan_body, nb=nb, tc=tc),
        grid=(batch // nb, t_len // tc),
        in_specs=[seq_spec] * 6 + [st_spec, const(bd), const(eye), const(eye_far)],
        out_specs=[seq_spec, st_spec],
        out_shape=[jax.ShapeDtypeStruct(r.shape, F32), jax.ShapeDtypeStruct(s0.shape, F32)],
        scratch_shapes=[pltpu.VMEM((n_tiles, HEAD_DIM, LANES), F32)] + [chunk_scr] * 7
        + [pltpu.VMEM((n_tiles, tc + SUBLANES, LANES), F32), pltpu.VMEM((n_tiles // 2 * HEAD_DIM, 2 * LANES), BF16)],
        compiler_params=_cparams(2), name="wkv_scan")(r, w, k, v, a, b, s0, bd, eye, eye_far)


def _sink_softmax(sc, valid, sink):
    sm = jnp.where(valid, sc, NEG_INF)
    mx = jnp.maximum(jnp.max(sm, axis=-1, keepdims=True), sink)
    p = jnp.exp(sm - mx)
    return p / (jnp.sum(p, axis=-1, keepdims=True) + jnp.exp(sink - mx))


def _attn_prompt_body(sink_ref, q_ref, kp_ref, kc_ref, vp_ref, vc_ref, o_ref, *, front):
    n = pl.program_id(1)
    keys = jnp.concatenate([kp_ref[...], kc_ref[...]], axis=0)
    vals = jnp.concatenate([vp_ref[...], vc_ref[...]], axis=0)
    shp = (ATT_BLOCK, 2 * ATT_BLOCK)
    qpos = n * ATT_BLOCK - front + lax.broadcasted_iota(jnp.int32, shp, 0)
    kpos = (n - 1) * ATT_BLOCK - front + lax.broadcasted_iota(jnp.int32, shp, 1)
    dlt = qpos - kpos
    valid = (kpos >= 0) & (dlt >= 0) & (dlt < WINDOW)
    half = lax.broadcasted_iota(jnp.int32, (ATT_BLOCK, LANES), 1) // HEAD_DIM
    for s in range(ATTN_WIDTH // LANES):
        q = q_ref[:, s * LANES:(s + 1) * LANES]
        acc = jnp.zeros((ATT_BLOCK, LANES), F32)
        for g in range(KV_HEADS):
            sc = _dot3_nt(jnp.where(half == g, q, 0.0), keys)
            p = _sink_softmax(sc, valid, sink_ref[s + g * (ATTN_HEADS // KV_HEADS)])
            ph, plo = _split2(p)
            vh, vl = _split2(vals)
            o = _dot(ph, vh) + (_dot(plo, vh) + _dot(ph, vl))
            acc = acc + jnp.where(half == g, o, 0.0)
        o_ref[:, s * LANES:(s + 1) * LANES] = acc


def _attn_prompt(sinks, q, k, v, *, front):
    batch, t_len = q.shape[0], q.shape[1]
    blk = lambda w, f: pl.BlockSpec((None, ATT_BLOCK, w), f)
    cur = lambda b, n: (b, n, 0)
    prv = lambda b, n: (b, jnp.maximum(n - 1, 0), 0)
    return pl.pallas_call(
        functools.partial(_attn_prompt_body, front=front),
        grid=(batch, t_len // ATT_BLOCK),
        in_specs=[pl.BlockSpec(memory_space=pltpu.SMEM), blk(ATTN_WIDTH, cur),
                  blk(KV_WIDTH, prv), blk(KV_WIDTH, cur), blk(KV_WIDTH, prv), blk(KV_WIDTH, cur)],
        out_specs=blk(ATTN_WIDTH, cur),
        out_shape=jax.ShapeDtypeStruct(q.shape, F32),
        compiler_params=_cparams(2), name="attn_prompt")(sinks, q, k, k, v, v)


def _attn_sample_body(sink_ref, q_ref, kn_ref, vn_ref, kb_ref, vb_ref, o_ref, *, bb, n_new):
    keys = jnp.concatenate([kb_ref[...], kn_ref[...]], axis=1)
    vals = jnp.concatenate([vb_ref[...], vn_ref[...]], axis=1)
    n_keys = WINDOW + SUBLANES
    shp = (n_new, n_keys)
    tq = lax.broadcasted_iota(jnp.int32, shp, 0)
    sk = lax.broadcasted_iota(jnp.int32, shp, 1)
    dlt = tq - (sk - WINDOW)
    valid = ((dlt >= 0) & (dlt < WINDOW))[None]
    half = lax.broadcasted_iota(jnp.int32, (bb, n_new, LANES), 2) // HEAD_DIM
    bdot = lambda x, y, cd: lax.dot_general(x, y, ((cd, ((0,), (0,)))), preferred_element_type=F32)
    kh, kl = _split2(keys)
    vh, vl = _split2(vals)
    for s in range(ATTN_WIDTH // LANES):
        q = q_ref[:, :, s * LANES:(s + 1) * LANES]
        acc = jnp.zeros((bb, n_new, LANES), F32)
        for g in range(KV_HEADS):
            qh, ql = _split2(jnp.where(half == g, q, 0.0))
            nt = ((2,), (2,))
            sc = bdot(qh, kh, nt) + (bdot(ql, kh, nt) + bdot(qh, kl, nt))
            p = _sink_softmax(sc, valid, sink_ref[s + g * (ATTN_HEADS // KV_HEADS)])
            ph, plo = _split2(p)
            nn = ((2,), (1,))
            o = bdot(ph, vh, nn) + (bdot(plo, vh, nn) + bdot(ph, vl, nn))
            acc = acc + jnp.where(half == g, o, 0.0)
        o_ref[:, :, s * LANES:(s + 1) * LANES] = acc


def _attn_sample(sinks, q, k_new, v_new, k_buf, v_buf, *, bb):
    batch, n_new = q.shape[0], q.shape[1]
    spec = lambda a: pl.BlockSpec((bb,) + a.shape[1:], lambda i: (i, 0, 0))
    return pl.pallas_call(
        functools.partial(_attn_sample_body, bb=bb, n_new=n_new),
        grid=(batch // bb,),
        in_specs=[pl.BlockSpec(memory_space=pltpu.SMEM)] + [spec(a) for a in (q, k_new, v_new, k_buf, v_buf)],
        out_specs=spec(q),
        out_shape=jax.ShapeDtypeStruct(q.shape, F32),
        compiler_params=_cparams(1), name="attn_sample")(sinks, q, k_new, v_new, k_buf, v_buf)


def _mix_body(x_ref, y_ref, g_ref, bon_ref, ya_ref, lw_ref, lb_ref, bd_ref, wo_ref, n2_ref, wq_ref,
              skh_ref, skl_ref, x1_o, h2_o, sc_o):
    bd = bd_ref[...]
    y = y_ref[...]
    inv = 1.0 / HEAD_DIM
    mean = _dot_exact01(y, bd) * inv
    yc = y - mean
    var = _dot_exact01(yc * yc, bd) * inv
    yn = yc * lax.rsqrt(var + LNX_EPS) * lw_ref[...] + lb_ref[...]
    yrw = (yn + bon_ref[...]) * g_ref[...]
    x1 = x_ref[...] + (_dot(yrw.astype(BF16), wo_ref[:RWKV_WIDTH, :]) + _dot(ya_ref[...].astype(BF16), wo_ref[RWKV_WIDTH:, :]))
    x1_o[...] = x1
    h2 = _rms(x1, n2_ref[...]).astype(BF16)
    h2_o[...] = h2
    qp = _dot(h2, wq_ref[...])
    for hc in range(2 * PEER_HEADS):
        c = hc % 2
        qh, ql = _split2(qp[:, hc * D_HALF:(hc + 1) * D_HALF])
        sc_o[hc] = _dot_nt(skh_ref[c], qh) + (_dot_nt(skh_ref[c], ql) + _dot_nt(skl_ref[c], qh))


def _mix(x, y, g, bon, ya, params, *, tm, skip):
    batch, t_len = x.shape[0], x.shape[1]
    n_out = t_len // tm - skip
    rows = batch * n_out * tm
    src = lambda w: pl.BlockSpec((None, tm, w), lambda b, j: (b, j + skip, 0))
    dst = lambda w: pl.BlockSpec((tm, w), lambda b, j: (b * n_out + j, 0))
    const = lambda a: pl.BlockSpec(a.shape, lambda b, j: (0,) * a.ndim)
    return pl.pallas_call(
        _mix_body, grid=(batch, n_out),
        in_specs=[src(D_MODEL), src(RWKV_WIDTH), src(RWKV_WIDTH), src(RWKV_WIDTH), src(ATTN_WIDTH)]
        + [const(a) for a in params],
        out_specs=[dst(D_MODEL), dst(D_MODEL),
                   pl.BlockSpec((2 * PEER_HEADS, N_KEYS, tm), lambda b, j: (0, 0, b * n_out + j))],
        out_shape=[jax.ShapeDtypeStruct((rows, D_MODEL), F32),
                   jax.ShapeDtypeStruct((rows, D_MODEL), BF16),
                   jax.ShapeDtypeStruct((2 * PEER_HEADS, N_KEYS, rows), F32)],
        compiler_params=_cparams(2), name="mix")(x, y, g, bon, ya, *params)


def _pair_tables(tt):
    pairs = [(k1, k2) for k1 in range(PEER_TOPK) for k2 in range(PEER_TOPK) if (k1 + 1) * (k2 + 1) <= PEER_TOPK]
    n_rows = -(-len(pairs) // SUBLANES) * SUBLANES
    g1 = np.zeros((n_rows, N_KEYS), np.float32)
    g2 = np.zeros((n_rows, N_KEYS), np.float32)
    neg = np.full((n_rows, tt), -np.inf, np.float32)
    flat = np.full((n_rows, tt), float(PEER_TOPK * PEER_TOPK), np.float32)
    for r, (k1, k2) in enumerate(pairs):
        g1[r, k1] = 1.0
        g2[r, k2] = 1.0
        neg[r] = 0.0
        flat[r] = k1 * PEER_TOPK + k2
    return jnp.asarray(g1, dtype=BF16), jnp.asarray(g2, dtype=BF16), jnp.asarray(neg), jnp.asarray(flat)


def _topk_body(s_ref, g1_ref, g2_ref, neg_ref, flat_ref, ia_o, ib_o, gt_o, v_s, i_s, a_s, b_s, t_s, *, tt):
    key = lax.broadcasted_iota(jnp.int32, (N_KEYS, tt), 0).astype(F32)
    rank = lax.broadcasted_iota(jnp.int32, (PEER_TOPK, tt), 0).astype(F32)
    neg, flat = neg_ref[...], flat_ref[...]
    v_s[...] = jnp.zeros((2, N_KEYS, tt), F32)

    def expand(g, v):
        hi, mid, lo = _split3(v)
        return _dot(g, hi) + _dot(g, mid) + _dot(g, lo)

    def head(h, carry):
        for c in range(2):
            x = s_ref[2 * h + c]
            for it in range(PEER_TOPK):
                m = jnp.max(x, axis=0, keepdims=True)
                am = jnp.min(jnp.where(x == m, key, float(N_KEYS)), axis=0, keepdims=True)
                v_s[c, it:it + 1, :] = m
                i_s[c, it:it + 1, :] = am
                x = jnp.where(key == am, -jnp.inf, x)
        cand = expand(g1_ref[...], v_s[0]) + expand(g2_ref[...], v_s[1]) + neg
        i1, i2 = i_s[0], i_s[1]
        base = pl.multiple_of(h * PEER_TOPK, PEER_TOPK)
        for it in range(PEER_TOPK):
            m = jnp.max(cand, axis=0, keepdims=True)
            fl = jnp.min(jnp.where(cand == m, flat, float(PEER_TOPK * PEER_TOPK)), axis=0, keepdims=True)
            k1 = jnp.floor(fl * (1.0 / PEER_TOPK))
            k2 = fl - k1 * PEER_TOPK
            a_s[pl.ds(base + it, 1), :] = jnp.sum(jnp.where(rank == k1, i1, 0.0), axis=0, keepdims=True)
            b_s[pl.ds(base + it, 1), :] = jnp.sum(jnp.where(rank == k2, i2, 0.0), axis=0, keepdims=True)
            t_s[pl.ds(base + it, 1), :] = m
            cand = jnp.where(flat == fl, -jnp.inf, cand)
        top = t_s[pl.ds(base, PEER_TOPK), :]
        e = jnp.exp(top - top[0:1])
        t_s[pl.ds(base, PEER_TOPK), :] = e / jnp.sum(e, axis=0, keepdims=True)
        return carry

    lax.fori_loop(0, PEER_HEADS, head, 0)
    ia_o[...] = a_s[...].T.astype(jnp.int32)
    ib_o[...] = b_s[...].T.astype(jnp.int32)
    gt_o[...] = t_s[...].T


def _topk(scores_t, *, tt):
    rows = scores_t.shape[2]
    tabs = _pair_tables(tt)
    out = pl.BlockSpec((tt, N_SEL), lambda i: (i, 0))
    const = lambda a: pl.BlockSpec(a.shape, lambda i: (0,) * a.ndim)
    sel_scr = pltpu.VMEM((N_SEL, tt), F32)
    return pl.pallas_call(
        functools.partial(_topk_body, tt=tt), grid=(rows // tt,),
        in_specs=[pl.BlockSpec((2 * PEER_HEADS, N_KEYS, tt), lambda i: (0, 0, i))] + [const(a) for a in tabs],
        out_specs=[out, out, out],
        out_shape=[jax.ShapeDtypeStruct((rows, N_SEL), jnp.int32), jax.ShapeDtypeStruct((rows, N_SEL), jnp.int32),
                   jax.ShapeDtypeStruct((rows, N_SEL), F32)],
        scratch_shapes=[pltpu.VMEM((2, N_KEYS, tt), F32), pltpu.VMEM((2, PEER_TOPK, tt), F32), sel_scr, sel_scr, sel_scr],
        compiler_params=_cparams(1), name="peer_topk")(scores_t, *tabs)


def _peer_body(h_ref, ia_ref, ib_ref, gt_ref, x1_ref, fg_ref, eu_ref, ev_ref, o_ref, gate_s, acc_s, *, tm, n_a):
    e = pl.program_id(1)
    steps_per_phase = GATE_ROWS // n_a
    phase = e // steps_per_phase

    @pl.when(e == 0)
    def _():
        acc_s[...] = jnp.zeros((tm, D_MODEL), F32)

    @pl.when(e % steps_per_phase == 0)
    def _():
        first = phase * GATE_ROWS + lax.broadcasted_iota(jnp.int32, (GATE_ROWS, N_SEL), 0)
        second = lax.broadcasted_iota(jnp.int32, (N_KEYS, N_SEL), 0)

        def build(t, c):
            pa = jnp.where(first == ia_ref[pl.ds(t, 1), :], gt_ref[pl.ds(t, 1), :], 0.0).astype(BF16)
            qb = jnp.where(second == ib_ref[pl.ds(t, 1), :], 1.0, 0.0).astype(BF16)
            gate_s[pl.ds(pl.multiple_of(t * GATE_PITCH, SUBLANES), GATE_ROWS), :] = _dot_nt(pa, qb)
            return c

        lax.fori_loop(0, tm, build, 0, unroll=GATE_BUILD_UNROLL)

    pre = _dot(h_ref[...], eu_ref[...])
    acts = []
    for ai in range(n_a):
        x = pre[:, ai * N_KEYS:(ai + 1) * N_KEYS]
        gate = gate_s[pl.ds((e % steps_per_phase) * n_a + ai, tm, stride=GATE_PITCH), :]
        acts.append((0.5 * x * (1.0 + lax.erf(x * np.float32(np.sqrt(0.5)))) * gate).astype(BF16))
    acc_s[...] += _dot(jnp.concatenate(acts, axis=1), ev_ref[...])

    @pl.when(e == pl.num_programs(1) - 1)
    def _():
        o_ref[...] = _rms(x1_ref[...] + acc_s[...], fg_ref[...])


def _peer(h2, ia, ib, gt, x1, fg, eu_t, ev, *, tm, n_a):
    rows = h2.shape[0]
    eb = n_a * N_KEYS
    tok = lambda w: pl.BlockSpec((tm, w), lambda i, e: (i, 0))
    return pl.pallas_call(
        functools.partial(_peer_body, tm=tm, n_a=n_a),
        grid=(rows // tm, N_EXPERTS // eb),
        in_specs=[tok(D_MODEL), tok(N_SEL), tok(N_SEL), tok(N_SEL), tok(D_MODEL),
                  pl.BlockSpec((1, D_MODEL), lambda i, e: (0, 0)),
                  pl.BlockSpec((None, D_MODEL, eb), lambda i, e: (e, 0, 0)),
                  pl.BlockSpec((eb, D_MODEL), lambda i, e: (e, 0))],
        out_specs=tok(D_MODEL),
        out_shape=jax.ShapeDtypeStruct((rows, D_MODEL), F32),
        scratch_shapes=[pltpu.VMEM((tm * GATE_PITCH, N_KEYS), F32), pltpu.VMEM((tm, D_MODEL), F32)],
        compiler_params=_cparams(2), name="peer_experts")(h2, ia, ib, gt, x1, fg, eu_t, ev)


def _rope_tables(pos):
    half = ROT_DIM // 2
    inv_freq = ROPE_THETA ** (-jnp.arange(0, ROT_DIM, 2, dtype=F32) / ROT_DIM)
    ang = pos.astype(F32)[:, None] * inv_freq[None, :]
    cos, sin = jnp.cos(ang), jnp.sin(ang)
    n = pos.shape[0]
    ones = jnp.ones((n, HEAD_DIM - ROT_DIM), F32)
    zeros = jnp.zeros((n, HEAD_DIM - ROT_DIM), F32)
    zh = jnp.zeros((n, half), F32)
    c = jnp.concatenate([cos, cos, ones], axis=1)
    s_up = jnp.concatenate([-sin, zh, zeros], axis=1)
    s_dn = jnp.concatenate([zh, sin, zeros], axis=1)
    rep = LANES // HEAD_DIM
    return tuple(jnp.tile(t, (1, rep)) for t in (c, s_up, s_dn))


def _block_ones(n, blk):
    i = np.arange(n) // blk
    return jnp.asarray((i[:, None] == i[None, :]).astype(np.float32), dtype=BF16)


def _hi_lo(w):
    hi = w.astype(BF16)
    return hi, (w - hi.astype(F32)).astype(BF16)


def kernel(x_prompt, x_sample, cache_k_win, cache_v_win, state_wkv, state_shift, meta_tokens, norm1_g, w_in, mu_shift, w0, w_lora_w2, a0, w_lora_a2, w_lora_g2, k_k, k_a, r_k, lnx_w, lnx_b, attn_sinks, w_out, norm2_g, w_query, sub_keys, expert_u, expert_v, final_norm_g):
    batch, seq = x_prompt.shape[0], x_prompt.shape[1]
    dec_batch, dec_seq = x_sample.shape[0], x_sample.shape[1]
    l_real = N_META + seq
    front = (-l_real) % ATT_BLOCK
    l_pad = l_real + front
    head_rows = front + N_META
    assert head_rows % ATT_BLOCK == 0 and dec_seq <= SUBLANES and cache_k_win.shape[2] == WINDOW

    hpg = ATTN_HEADS // KV_HEADS
    qperm = np.concatenate([(s + hpg * g) * HEAD_DIM + np.arange(HEAD_DIM) for s in range(hpg) for g in range(KV_HEADS)])
    win = w_in[0]
    win = jnp.concatenate([win[:, :N_RW_COLS], win[:, N_RW_COLS + qperm], win[:, N_RW_COLS + ATTN_WIDTH:]], axis=1).astype(BF16)
    wout = w_out[0]
    wout = jnp.concatenate([wout[:RWKV_WIDTH], wout[RWKV_WIDTH + qperm]], axis=0).astype(BF16)
    sinks = attn_sinks[0]

    row = lambda a: a.reshape(1, -1)
    zpad = lambda w, top: jnp.concatenate([w, jnp.zeros_like(w)] if top else [jnp.zeros_like(w), w], axis=0)
    bd64 = _block_ones(RWKV_WIDTH, HEAD_DIM)
    prep_params = (row(mu_shift[0]), row(w0[0]), row(a0[0]), row(k_k[0]), row(k_a[0]), row(r_k[0]),
                   *_hi_lo(zpad(w_lora_w2[0], True)), *_hi_lo(zpad(w_lora_a2[0], False)), *_hi_lo(w_lora_g2[0]), bd64)
    mix_params = (row(lnx_w[0]), row(lnx_b[0]), bd64, wout, row(norm2_g[0]), w_query[0].astype(BF16), *_hi_lo(sub_keys[0]))
    n_a = 8
    eu_t = expert_u[0].astype(BF16).reshape(N_EXPERTS // (n_a * N_KEYS), n_a * N_KEYS, D_MODEL).transpose(0, 2, 1)
    ev = expert_v[0].astype(BF16)
    g1, fg = row(norm1_g[0]), row(final_norm_g)

    def ffn(x1, h2, scores_t):
        ia, ib, gt = _topk(scores_t, tt=LANES)
        return _peer(h2, ia, ib, gt, x1, fg, eu_t, ev, tm=min(PEER_TOKEN_TILE, x1.shape[0]), n_a=n_a)

    xp = jnp.concatenate([jnp.zeros((batch, front, D_MODEL), F32),
                          jnp.broadcast_to(meta_tokens[None], (batch, N_META, D_MODEL)), x_prompt], axis=1)
    proj = _inproj(xp.reshape(batch * l_pad, D_MODEL), g1, win, norm=True, tm=256)
    tabs = _rope_tables(jnp.arange(l_pad) - front)
    r, dec, kx, vx, av, bv, gg, bon, q, ka, va = _prep(proj, proj, tabs, prep_params, seq_rows=l_pad, tm=384, sample=False)
    b3 = lambda a: a.reshape(batch, l_pad, -1)
    y_rw, s_p = _scan(b3(r), b3(dec), b3(kx), b3(vx), b3(av), b3(bv),
                      jnp.zeros((batch * RWKV_HEADS, HEAD_DIM, HEAD_DIM), F32), nb=batch, tc=64)
    y_at = _attn_prompt(sinks, b3(q), b3(ka), b3(va), front=front)
    x1, h2, scores = _mix(xp, y_rw, b3(gg), b3(bon), y_at, mix_params, tm=ATT_BLOCK, skip=head_rows // ATT_BLOCK)
    y_prompt = ffn(x1, h2, scores).reshape(batch, seq, D_MODEL)
    keep = min(WINDOW, l_real)
    k_win_p = b3(ka)[:, l_pad - keep:].reshape(1, batch, keep, KV_HEADS, HEAD_DIM)
    v_win_p = b3(va)[:, l_pad - keep:].reshape(1, batch, keep, KV_HEADS, HEAD_DIM)
    wkv_p = s_p.reshape(1, batch, RWKV_HEADS, HEAD_DIM, HEAD_DIM)

    n_s = dec_batch * dec_seq
    xs = x_sample.reshape(n_s, D_MODEL)
    proj_s = _inproj(xs, g1, win, norm=True, tm=256)
    prev_proj = _inproj(state_shift[0], g1, win, norm=False, tm=dec_batch)
    prev_rows = jnp.pad(prev_proj[:, None, :], ((0, 0), (0, dec_seq - 1), (0, 0))).reshape(n_s, N_IN_COLS)
    pos_s = jnp.tile(PAST_LEN + jnp.arange(dec_seq), dec_batch)
    r, dec, kx, vx, av, bv, gg, bon, q, ka, va = _prep(proj_s, prev_rows, _rope_tables(pos_s), prep_params,
                                                       seq_rows=dec_seq, tm=n_s, sample=True)
    s3 = lambda a: a.reshape(dec_batch, dec_seq, -1)
    y_rw, s_s = _scan(s3(r), s3(dec), s3(kx), s3(vx), s3(av), s3(bv),
                      state_wkv[0].reshape(dec_batch * RWKV_HEADS, HEAD_DIM, HEAD_DIM), nb=4, tc=dec_seq)
    k_buf = cache_k_win[0].reshape(dec_batch, WINDOW, KV_WIDTH)
    v_buf = cache_v_win[0].reshape(dec_batch, WINDOW, KV_WIDTH)
    pad8 = lambda a: jnp.pad(a, ((0, 0), (0, SUBLANES - dec_seq), (0, 0)))
    y_at = _attn_sample(sinks, s3(q), pad8(s3(ka)), pad8(s3(va)), k_buf, v_buf, bb=8)
    one = lambda a: a.reshape(1, n_s, -1)
    x1, h2, scores = _mix(one(xs), one(y_rw), one(gg), one(bon), one(y_at), mix_params, tm=ATT_BLOCK, skip=0)
    y_sample = ffn(x1, h2, scores).reshape(dec_batch, dec_seq, D_MODEL)
    k_win_s = jnp.concatenate([k_buf, s3(ka)], axis=1)[:, -WINDOW:].reshape(1, dec_batch, WINDOW, KV_HEADS, HEAD_DIM)
    v_win_s = jnp.concatenate([v_buf, s3(va)], axis=1)[:, -WINDOW:].reshape(1, dec_batch, WINDOW, KV_HEADS, HEAD_DIM)
    wkv_s = s_s.reshape(1, dec_batch, RWKV_HEADS, HEAD_DIM, HEAD_DIM)

    last = jnp.concatenate([x_prompt[:, -1], x_sample[:, -1]], axis=0)
    pad_rows = (-last.shape[0]) % SUBLANES
    shift = _rmsnorm_rows(jnp.pad(last, ((0, pad_rows), (0, 0))), g1)
    shift_p = shift[:batch].reshape(1, batch, D_MODEL)
    shift_s = shift[batch:batch + dec_batch].reshape(1, dec_batch, D_MODEL)

    return (y_prompt, y_sample, k_win_p, v_win_p, wkv_p, shift_p, k_win_s, v_win_s, wkv_s, shift_s)
```

```python
import functools

import numpy as np
import jax
import jax.numpy as jnp
from jax import lax
from jax.experimental import pallas as pl
from jax.experimental.pallas import tpu as pltpu

F32 = jnp.float32
BF16 = jnp.bfloat16

D_MODEL = 1024
HEAD_DIM = 64
RWKV_WIDTH = 512
RWKV_HEADS = 8
ATTN_WIDTH = 512
ATTN_HEADS = 8
KV_HEADS = 2
KV_WIDTH = KV_HEADS * HEAD_DIM
DECAY_LORA = 64
AAA_LORA = 64
GATE_LORA = 128
N_RW_COLS = 3 * RWKV_WIDTH + DECAY_LORA + AAA_LORA + GATE_LORA
N_IN_COLS = N_RW_COLS + ATTN_WIDTH + 2 * KV_WIDTH
LNX_EPS = 64e-5
WINDOW = 128
ATT_BLOCK = 128
ATTN_SCALE = HEAD_DIM ** -0.5
ROPE_THETA = 500000.0
ROT_DIM = HEAD_DIM // 4
N_META = 16
N_KEYS = 128
N_EXPERTS = N_KEYS * N_KEYS
PEER_HEADS = 8
PEER_TOPK = 16
N_SEL = PEER_HEADS * PEER_TOPK
D_HALF = 128
NORM_EPS = 1e-5
NEG_INF = -1e30
PAST_LEN = 8192

LANES = 128
SUBLANES = 8
GATE_PITCH = N_KEYS + SUBLANES
VMEM_LIMIT = 56 * 1024 * 1024
GATE_BUILD_UNROLL = 16
TOPK_TOKEN_TILE = 2 * LANES
PEER_TOKEN_TILE = 256


def _cparams(n_axes):
    return pltpu.CompilerParams(dimension_semantics=("arbitrary",) * n_axes, vmem_limit_bytes=VMEM_LIMIT)


def _split2(x):
    hi = x.astype(BF16)
    lo = (x - hi.astype(F32)).astype(BF16)
    return hi, lo


def _split3(x):
    hi = x.astype(BF16)
    r1 = x - hi.astype(F32)
    mid = r1.astype(BF16)
    lo = (r1 - mid.astype(F32)).astype(BF16)
    return hi, mid, lo


def _dot(a, b):
    return jnp.dot(a, b, preferred_element_type=F32)


def _dot_nt(a, b):
    return lax.dot_general(a, b, (((1,), (1,)), ((), ())), preferred_element_type=F32)


def _dot_exact01(x, m01):
    hi, mid, lo = _split3(x)
    return _dot(hi, m01) + _dot(mid, m01) + _dot(lo, m01)


def _dot3(x, wh, wl):
    xh, xl = _split2(x)
    return _dot(xh, wh) + (_dot(xl, wh) + _dot(xh, wl))


def _dot3_nt(x, y):
    xh, xl = _split2(x)
    yh, yl = _split2(y)
    return _dot_nt(xh, yh) + (_dot_nt(xl, yh) + _dot_nt(xh, yl))


def _rms(x, g):
    return x * lax.rsqrt(jnp.mean(x * x, axis=-1, keepdims=True) + NORM_EPS) * g


def _inproj_body(x_ref, g_ref, w_ref, o_ref, *, norm):
    x = x_ref[...]
    if norm:
        x = _rms(x, g_ref[...])
    o_ref[...] = _dot(x.astype(BF16), w_ref[...])


def _inproj(x, g, w_bf, *, norm, tm):
    rows, ncol = x.shape[0], w_bf.shape[1]
    return pl.pallas_call(
        functools.partial(_inproj_body, norm=norm),
        grid=(rows // tm,),
        in_specs=[pl.BlockSpec((tm, D_MODEL), lambda i: (i, 0)),
                  pl.BlockSpec((1, D_MODEL), lambda i: (0, 0)),
                  pl.BlockSpec((D_MODEL, ncol), lambda i: (0, 0))],
        out_specs=pl.BlockSpec((tm, ncol), lambda i: (i, 0)),
        out_shape=jax.ShapeDtypeStruct((rows, ncol), F32),
        compiler_params=_cparams(1), name="inproj")(x, g, w_bf)


def _rmsnorm_body(x_ref, g_ref, o_ref):
    o_ref[...] = _rms(x_ref[...], g_ref[...])


def _rmsnorm_rows(x, g):
    rows = x.shape[0]
    return pl.pallas_call(
        _rmsnorm_body, grid=(1,),
        in_specs=[pl.BlockSpec((rows, D_MODEL), lambda i: (0, 0)), pl.BlockSpec((1, D_MODEL), lambda i: (0, 0))],
        out_specs=pl.BlockSpec((rows, D_MODEL), lambda i: (0, 0)),
        out_shape=jax.ShapeDtypeStruct((rows, D_MODEL), F32),
        compiler_params=_cparams(1), name="rmsnorm_rows")(x, g)


def _rope(x, c, s_up, s_dn):
    up = pltpu.roll(x, LANES - ROT_DIM // 2, 1)
    dn = pltpu.roll(x, ROT_DIM // 2, 1)
    return x * c + up * s_up + dn * s_dn


def _prep_body(*refs, seq_rows, tile_rows, sample):
    if sample:
        p_ref, prev_ref = refs[0], refs[1]
    else:
        p_ref, prev8_ref = refs[0], refs[1]
    (cos_ref, sup_ref, sdn_ref, mu_ref, w0_ref, a0_ref, kk_ref, ka_ref, rk_ref,
     w2h_ref, w2l_ref, a2h_ref, a2l_ref, g2h_ref, g2l_ref, bd_ref) = refs[2:18]
    r_o, dec_o, k_o, v_o, av_o, bv_o, g_o, bon_o, q_o, ka_o, va_o = refs[18:]

    prw = p_ref[:, :N_RW_COLS]
    rolled = pltpu.roll(prw, 1, 0)
    row = lax.broadcasted_iota(jnp.int32, (tile_rows, 1), 0)
    if sample:
        pprev = jnp.where(row % seq_rows == 0, prev_ref[:, :N_RW_COLS], rolled)
    else:
        tiles_per_seq = seq_rows // tile_rows
        at_start = (pl.program_id(0) % tiles_per_seq) == 0
        first = jnp.where(at_start, 0.0, prev8_ref[SUBLANES - 1:SUBLANES, :N_RW_COLS])
        pprev = jnp.where(row == 0, first, rolled)
    m = prw + (pprev - prw) * mu_ref[...]

    c = RWKV_WIDTH
    xr, xk, xv = m[:, :c], m[:, c:2 * c], m[:, 2 * c:3 * c]
    xwa = m[:, 3 * c:3 * c + LANES]
    xg = m[:, 3 * c + LANES:3 * c + 2 * LANES]

    lw = w0_ref[...] + _dot3(jnp.tanh(xwa), w2h_ref[...], w2l_ref[...])
    z = -lw
    w = -(jnp.maximum(z, 0.0) + jnp.log1p(jnp.exp(-jnp.abs(z)))) - 0.5
    dec_o[...] = jnp.exp(-jnp.exp(w))
    a = jax.nn.sigmoid(a0_ref[...] + _dot3(xwa, a2h_ref[...], a2l_ref[...]))
    g_o[...] = _dot3(jax.nn.sigmoid(xg), g2h_ref[...], g2l_ref[...])

    bd = bd_ref[...]
    kk = xk * kk_ref[...]
    kk = kk / jnp.maximum(jnp.sqrt(_dot_exact01(kk * kk, bd)), 1e-12)
    kmod = xk * (1.0 + (a - 1.0) * ka_ref[...])
    r_o[...] = xr
    k_o[...] = kmod
    v_o[...] = xv
    av_o[...] = -kk
    bv_o[...] = kk * a
    bon_o[...] = _dot_exact01(xr * kmod * rk_ref[...], bd) * xv

    cs, su, sd = cos_ref[...], sup_ref[...], sdn_ref[...]
    o = N_RW_COLS
    for s in range(ATTN_WIDTH // LANES):
        q_o[:, s * LANES:(s + 1) * LANES] = _rope(p_ref[:, o + s * LANES:o + (s + 1) * LANES], cs, su, sd) * ATTN_SCALE
    o += ATTN_WIDTH
    ka_o[...] = _rope(p_ref[:, o:o + KV_WIDTH], cs, su, sd)
    va_o[...] = p_ref[:, o + KV_WIDTH:o + 2 * KV_WIDTH]


def _prep(p, prev, tabs, params, *, seq_rows, tm, sample):
    rows = p.shape[0]
    n_tiles = rows // tm
    row_spec = lambda w: pl.BlockSpec((tm, w), lambda i: (i, 0))
    const = lambda a: pl.BlockSpec(a.shape, lambda i: (0,) * a.ndim)
    if sample:
        prev_spec = row_spec(N_IN_COLS)
        tab_spec = row_spec(LANES)
    else:
        blocks = tm // SUBLANES
        prev_spec = pl.BlockSpec((SUBLANES, N_IN_COLS), lambda i: (jnp.maximum(i * blocks - 1, 0), 0))
        tiles_per_seq = seq_rows // tm
        tab_spec = pl.BlockSpec((tm, LANES), lambda i: (i % tiles_per_seq, 0))
    in_specs = [row_spec(N_IN_COLS), prev_spec, tab_spec, tab_spec, tab_spec] + [const(a) for a in params]
    widths = [RWKV_WIDTH] * 8 + [ATTN_WIDTH, KV_WIDTH, KV_WIDTH]
    return pl.pallas_call(
        functools.partial(_prep_body, seq_rows=seq_rows, tile_rows=tm, sample=sample),
        grid=(n_tiles,), in_specs=in_specs,
        out_specs=[row_spec(w) for w in widths],
        out_shape=[jax.ShapeDtypeStruct((rows, w), F32) for w in widths],
        compiler_params=_cparams(1), name="prep")(p, prev, *tabs, *params)


def _scan_body(r_ref, w_ref, k_ref, v_ref, a_ref, b_ref, s0_ref, bd_ref, eye_ref, esh_ref, y_ref, so_ref,
               st_s, *row_s, nb, tc):
    ci = pl.program_id(1)
    slabs = RWKV_WIDTH // LANES
    n_tiles = nb * slabs
    n_pairs = n_tiles // 2
    heads_per_tile = LANES // HEAD_DIM
    r_s, w_s, k_s, vh_s, a_s, b_s, vl_s, y_s, p2_s = row_s

    def tile_states(p):
        return [((p // slabs) * RWKV_HEADS + heads_per_tile * (p % slabs) + m, m * HEAD_DIM) for m in range(heads_per_tile)]

    @pl.when(ci == 0)
    def _():
        for p in range(n_tiles):
            for s, off in tile_states(p):
                st_s[p, :, off:off + HEAD_DIM] = s0_ref[s]

        p2_s[...] = jnp.zeros(p2_s.shape, BF16)

    bd = bd_ref[...]
    eye = eye_ref[...]
    eye_bf = eye.astype(BF16)
    far_bf = esh_ref[...].astype(BF16)

    half_head = HEAD_DIM // 2
    first_half = lax.broadcasted_iota(jnp.int32, (tc, LANES), 1) % HEAD_DIM < half_head
    for p in range(n_tiles):
        sl = (p // slabs, slice(None), slice((p % slabs) * LANES, (p % slabs + 1) * LANES))
        for src, dst in ((r_ref, r_s), (w_ref, w_s), (k_ref, k_s), (a_ref, a_s), (b_ref, b_s)):
            dst[p] = src[sl]
        v = v_ref[sl]
        hi = v.astype(BF16).astype(F32)
        lo = (v - hi).astype(BF16).astype(F32)
        vh_s[p] = hi
        vl_s[p] = jnp.where(first_half, pltpu.roll(lo, LANES - half_head, 1), pltpu.roll(lo, half_head, 1))

    def row(ref, t, p):
        return ref[p, pl.ds(t, 1), :]

    def paired(tiles):
        return jnp.concatenate([jnp.concatenate([tiles[p], tiles[p + n_pairs]], axis=1) for p in range(n_pairs)], axis=0)

    def unpaired(x):
        rows = lambda p: slice((p % n_pairs) * HEAD_DIM, (p % n_pairs + 1) * HEAD_DIM)
        return [x[rows(p), (p // n_pairs) * LANES:(p // n_pairs + 1) * LANES] for p in range(n_tiles)]

    def head_sums(tiles):
        return unpaired(_dot(paired([x.astype(BF16) for x in tiles]), bd))

    def emit_y(row_idx):
        yc = unpaired(_dot(p2_s[...], bd))
        for p in range(n_tiles):
            y_s[p, pl.ds(row_idx, 1), :] = jnp.sum(yc[p] * eye, axis=0, keepdims=True)

    def step(t, carry):
        emit_y(t + (SUBLANES - 1))
        st = [st_s[p] for p in range(n_tiles)]
        sa = head_sums([st[p] * row(a_s, t, p) for p in range(n_tiles)])
        vd = [eye_bf * row(vh_s, t, p).astype(BF16) + far_bf * row(vl_s, t, p).astype(BF16)
              for p in range(n_tiles)]
        vc = head_sums(vd)
        p2 = []
        for p in range(n_tiles):
            new = st[p] * row(w_s, t, p) + sa[p] * row(b_s, t, p) + vc[p] * row(k_s, t, p)
            st_s[p] = new
            p2.append((new * row(r_s, t, p)).astype(BF16))
        p2_s[...] = paired(p2)
        return carry

    lax.fori_loop(0, tc, step, 0)
    emit_y(tc + (SUBLANES - 1))
    for p in range(n_tiles):
        y_ref[p // slabs, :, (p % slabs) * LANES:(p % slabs + 1) * LANES] = y_s[p, SUBLANES:, :]

    @pl.when(ci == pl.num_programs(1) - 1)
    def _():
        for p in range(n_tiles):
            for s, off in tile_states(p):
                so_ref[s] = st_s[p, :, off:off + HEAD_DIM]


def _scan(r, w, k, v, a, b, s0, *, nb, tc):
    batch, t_len = r.shape[0], r.shape[1]
    ns = nb * RWKV_HEADS
    n_tiles = nb * RWKV_WIDTH // LANES
    bd = _block_ones(2 * LANES, HEAD_DIM)
    lane_in_head = np.arange(LANES)[None, :] % HEAD_DIM
    value_row = np.arange(HEAD_DIM)[:, None]
    eye = jnp.asarray(lane_in_head == value_row, dtype=F32)
    eye_far = jnp.asarray(lane_in_head == (value_row + HEAD_DIM // 2) % HEAD_DIM, dtype=F32)
    seq_spec = pl.BlockSpec((nb, tc, RWKV_WIDTH), lambda g, c: (g, c, 0))
    st_spec = pl.BlockSpec((ns, HEAD_DIM, HEAD_DIM), lambda g, c: (g, 0, 0))
    const = lambda x: pl.BlockSpec(x.shape, lambda g, c: (0,) * x.ndim)
    chunk_scr = pltpu.VMEM((n_tiles, tc, LANES), F32)
    return pl.pallas_call(
        functools.partial(_scan_body, nb=nb, tc=tc),
        grid=(batch // nb, t_len // tc),
        in_specs=[seq_spec] * 6 + [st_spec, const(bd), const(eye), const(eye_far)],
        out_specs=[seq_spec, st_spec],
        out_shape=[jax.ShapeDtypeStruct(r.shape, F32), jax.ShapeDtypeStruct(s0.shape, F32)],
        scratch_shapes=[pltpu.VMEM((n_tiles, HEAD_DIM, LANES), F32)] + [chunk_scr] * 7
        + [pltpu.VMEM((n_tiles, tc + SUBLANES, LANES), F32), pltpu.VMEM((n_tiles // 2 * HEAD_DIM, 2 * LANES), BF16)],
        compiler_params=_cparams(2), name="wkv_scan")(r, w, k, v, a, b, s0, bd, eye, eye_far)


def _sink_softmax(sc, valid, sink):
    sm = jnp.where(valid, sc, NEG_INF)
    mx = jnp.maximum(jnp.max(sm, axis=-1, keepdims=True), sink)
    p = jnp.exp(sm - mx)
    return p / (jnp.sum(p, axis=-1, keepdims=True) + jnp.exp(sink - mx))


def _attn_prompt_body(sink_ref, q_ref, kp_ref, kc_ref, vp_ref, vc_ref, o_ref, *, front):
    n = pl.program_id(1)
    keys = jnp.concatenate([kp_ref[...], kc_ref[...]], axis=0)
    vals = jnp.concatenate([vp_ref[...], vc_ref[...]], axis=0)
    shp = (ATT_BLOCK, 2 * ATT_BLOCK)
    qpos = n * ATT_BLOCK - front + lax.broadcasted_iota(jnp.int32, shp, 0)
    kpos = (n - 1) * ATT_BLOCK - front + lax.broadcasted_iota(jnp.int32, shp, 1)
    dlt = qpos - kpos
    valid = (kpos >= 0) & (dlt >= 0) & (dlt < WINDOW)
    half = lax.broadcasted_iota(jnp.int32, (ATT_BLOCK, LANES), 1) // HEAD_DIM
    for s in range(ATTN_WIDTH // LANES):
        q = q_ref[:, s * LANES:(s + 1) * LANES]
        acc = jnp.zeros((ATT_BLOCK, LANES), F32)
        for g in range(KV_HEADS):
            sc = _dot3_nt(jnp.where(half == g, q, 0.0), keys)
            p = _sink_softmax(sc, valid, sink_ref[s + g * (ATTN_HEADS // KV_HEADS)])
            ph, plo = _split2(p)
            vh, vl = _split2(vals)
            o = _dot(ph, vh) + (_dot(plo, vh) + _dot(ph, vl))
            acc = acc + jnp.where(half == g, o, 0.0)
        o_ref[:, s * LANES:(s + 1) * LANES] = acc


def _attn_prompt(sinks, q, k, v, *, front):
    batch, t_len = q.shape[0], q.shape[1]
    blk = lambda w, f: pl.BlockSpec((None, ATT_BLOCK, w), f)
    cur = lambda b, n: (b, n, 0)
    prv = lambda b, n: (b, jnp.maximum(n - 1, 0), 0)
    return pl.pallas_call(
        functools.partial(_attn_prompt_body, front=front),
        grid=(batch, t_len // ATT_BLOCK),
        in_specs=[pl.BlockSpec(memory_space=pltpu.SMEM), blk(ATTN_WIDTH, cur),
                  blk(KV_WIDTH, prv), blk(KV_WIDTH, cur), blk(KV_WIDTH, prv), blk(KV_WIDTH, cur)],
        out_specs=blk(ATTN_WIDTH, cur),
        out_shape=jax.ShapeDtypeStruct(q.shape, F32),
        compiler_params=_cparams(2), name="attn_prompt")(sinks, q, k, k, v, v)


def _attn_sample_body(sink_ref, q_ref, kn_ref, vn_ref, kb_ref, vb_ref, o_ref, *, bb, n_new):
    keys = jnp.concatenate([kb_ref[...], kn_ref[...]], axis=1)
    vals = jnp.concatenate([vb_ref[...], vn_ref[...]], axis=1)
    n_keys = WINDOW + SUBLANES
    shp = (n_new, n_keys)
    tq = lax.broadcasted_iota(jnp.int32, shp, 0)
    sk = lax.broadcasted_iota(jnp.int32, shp, 1)
    dlt = tq - (sk - WINDOW)
    valid = ((dlt >= 0) & (dlt < WINDOW))[None]
    half = lax.broadcasted_iota(jnp.int32, (bb, n_new, LANES), 2) // HEAD_DIM
    bdot = lambda x, y, cd: lax.dot_general(x, y, ((cd, ((0,), (0,)))), preferred_element_type=F32)
    kh, kl = _split2(keys)
    vh, vl = _split2(vals)
    for s in range(ATTN_WIDTH // LANES):
        q = q_ref[:, :, s * LANES:(s + 1) * LANES]
        acc = jnp.zeros((bb, n_new, LANES), F32)
        for g in range(KV_HEADS):
            qh, ql = _split2(jnp.where(half == g, q, 0.0))
            nt = ((2,), (2,))
            sc = bdot(qh, kh, nt) + (bdot(ql, kh, nt) + bdot(qh, kl, nt))
            p = _sink_softmax(sc, valid, sink_ref[s + g * (ATTN_HEADS // KV_HEADS)])
            ph, plo = _split2(p)
            nn = ((2,), (1,))
            o = bdot(ph, vh, nn) + (bdot(plo, vh, nn) + bdot(ph, vl, nn))
            acc = acc + jnp.where(half == g, o, 0.0)
        o_ref[:, :, s * LANES:(s + 1) * LANES] = acc


def _attn_sample(sinks, q, k_new, v_new, k_buf, v_buf, *, bb):
    batch, n_new = q.shape[0], q.shape[1]
    spec = lambda a: pl.BlockSpec((bb,) + a.shape[1:], lambda i: (i, 0, 0))
    return pl.pallas_call(
        functools.partial(_attn_sample_body, bb=bb, n_new=n_new),
        grid=(batch // bb,),
        in_specs=[pl.BlockSpec(memory_space=pltpu.SMEM)] + [spec(a) for a in (q, k_new, v_new, k_buf, v_buf)],
        out_specs=spec(q),
        out_shape=jax.ShapeDtypeStruct(q.shape, F32),
        compiler_params=_cparams(1), name="attn_sample")(sinks, q, k_new, v_new, k_buf, v_buf)


def _mix_body(x_ref, y_ref, g_ref, bon_ref, ya_ref, lw_ref, lb_ref, bd_ref, wo_ref, n2_ref, wq_ref,
              skh_ref, skl_ref, x1_o, h2_o, sc_o):
    bd = bd_ref[...]
    y = y_ref[...]
    inv = 1.0 / HEAD_DIM
    mean = _dot_exact01(y, bd) * inv
    yc = y - mean
    var = _dot_exact01(yc * yc, bd) * inv
    yn = yc * lax.rsqrt(var + LNX_EPS) * lw_ref[...] + lb_ref[...]
    yrw = (yn + bon_ref[...]) * g_ref[...]
    x1 = x_ref[...] + (_dot(yrw.astype(BF16), wo_ref[:RWKV_WIDTH, :]) + _dot(ya_ref[...].astype(BF16), wo_ref[RWKV_WIDTH:, :]))
    x1_o[...] = x1
    h2 = _rms(x1, n2_ref[...]).astype(BF16)
    h2_o[...] = h2
    qp = _dot(h2, wq_ref[...])
    for hc in range(2 * PEER_HEADS):
        c = hc % 2
        qh, ql = _split2(qp[:, hc * D_HALF:(hc + 1) * D_HALF])
        sc_o[hc] = _dot_nt(skh_ref[c], qh) + (_dot_nt(skh_ref[c], ql) + _dot_nt(skl_ref[c], qh))


def _mix(x, y, g, bon, ya, params, *, tm, skip):
    batch, t_len = x.shape[0], x.shape[1]
    n_out = t_len // tm - skip
    rows = batch * n_out * tm
    src = lambda w: pl.BlockSpec((None, tm, w), lambda b, j: (b, j + skip, 0))
    dst = lambda w: pl.BlockSpec((tm, w), lambda b, j: (b * n_out + j, 0))
    const = lambda a: pl.BlockSpec(a.shape, lambda b, j: (0,) * a.ndim)
    return pl.pallas_call(
        _mix_body, grid=(batch, n_out),
        in_specs=[src(D_MODEL), src(RWKV_WIDTH), src(RWKV_WIDTH), src(RWKV_WIDTH), src(ATTN_WIDTH)]
        + [const(a) for a in params],
        out_specs=[dst(D_MODEL), dst(D_MODEL),
                   pl.BlockSpec((2 * PEER_HEADS, N_KEYS, tm), lambda b, j: (0, 0, b * n_out + j))],
        out_shape=[jax.ShapeDtypeStruct((rows, D_MODEL), F32),
                   jax.ShapeDtypeStruct((rows, D_MODEL), BF16),
                   jax.ShapeDtypeStruct((2 * PEER_HEADS, N_KEYS, rows), F32)],
        compiler_params=_cparams(2), name="mix")(x, y, g, bon, ya, *params)


def _pair_tables(tt):
    pairs = [(k1, k2) for k1 in range(PEER_TOPK) for k2 in range(PEER_TOPK) if (k1 + 1) * (k2 + 1) <= PEER_TOPK]
    n_rows = -(-len(pairs) // SUBLANES) * SUBLANES
    g1 = np.zeros((n_rows, N_KEYS), np.float32)
    g2 = np.zeros((n_rows, N_KEYS), np.float32)
    neg = np.full((n_rows, tt), -np.inf, np.float32)
    flat = np.full((n_rows, tt), float(PEER_TOPK * PEER_TOPK), np.float32)
    for r, (k1, k2) in enumerate(pairs):
        g1[r, k1] = 1.0
        g2[r, k2] = 1.0
        neg[r] = 0.0
        flat[r] = k1 * PEER_TOPK + k2
    return jnp.asarray(g1, dtype=BF16), jnp.asarray(g2, dtype=BF16), jnp.asarray(neg), jnp.asarray(flat)


def _topk_body(s_ref, g1_ref, g2_ref, neg_ref, flat_ref, ia_o, ib_o, gt_o, v_s, i_s, a_s, b_s, t_s, *, tt):
    key = lax.broadcasted_iota(jnp.int32, (N_KEYS, tt), 0).astype(F32)
    rank = lax.broadcasted_iota(jnp.int32, (PEER_TOPK, tt), 0).astype(F32)
    neg, flat = neg_ref[...], flat_ref[...]
    v_s[...] = jnp.zeros((2, N_KEYS, tt), F32)

    def expand(g, v):
        hi, mid, lo = _split3(v)
        return _dot(g, hi) + _dot(g, mid) + _dot(g, lo)

    def head(h, carry):
        for c in range(2):
            x = s_ref[2 * h + c]
            for it in range(PEER_TOPK):
                m = jnp.max(x, axis=0, keepdims=True)
                am = jnp.min(jnp.where(x == m, key, float(N_KEYS)), axis=0, keepdims=True)
                v_s[c, it:it + 1, :] = m
                i_s[c, it:it + 1, :] = am
                x = jnp.where(key == am, -jnp.inf, x)
        cand = expand(g1_ref[...], v_s[0]) + expand(g2_ref[...], v_s[1]) + neg
        i1, i2 = i_s[0], i_s[1]
        base = pl.multiple_of(h * PEER_TOPK, PEER_TOPK)
        for it in range(PEER_TOPK):
            m = jnp.max(cand, axis=0, keepdims=True)
            fl = jnp.min(jnp.where(cand == m, flat, float(PEER_TOPK * PEER_TOPK)), axis=0, keepdims=True)
            k1 = jnp.floor(fl * (1.0 / PEER_TOPK))
            k2 = fl - k1 * PEER_TOPK
            a_s[pl.ds(base + it, 1), :] = jnp.sum(jnp.where(rank == k1, i1, 0.0), axis=0, keepdims=True)
            b_s[pl.ds(base + it, 1), :] = jnp.sum(jnp.where(rank == k2, i2, 0.0), axis=0, keepdims=True)
            t_s[pl.ds(base + it, 1), :] = m
            cand = jnp.where(flat == fl, -jnp.inf, cand)
        top = t_s[pl.ds(base, PEER_TOPK), :]
        e = jnp.exp(top - top[0:1])
        t_s[pl.ds(base, PEER_TOPK), :] = e / jnp.sum(e, axis=0, keepdims=True)
        return carry

    lax.fori_loop(0, PEER_HEADS, head, 0)
    ia_o[...] = a_s[...].T.astype(jnp.int32)
    ib_o[...] = b_s[...].T.astype(jnp.int32)
    gt_o[...] = t_s[...].T


def _topk(scores_t, *, tt):
    rows = scores_t.shape[2]
    tabs = _pair_tables(tt)
    out = pl.BlockSpec((tt, N_SEL), lambda i: (i, 0))
    const = lambda a: pl.BlockSpec(a.shape, lambda i: (0,) * a.ndim)
    sel_scr = pltpu.VMEM((N_SEL, tt), F32)
    return pl.pallas_call(
        functools.partial(_topk_body, tt=tt), grid=(rows // tt,),
        in_specs=[pl.BlockSpec((2 * PEER_HEADS, N_KEYS, tt), lambda i: (0, 0, i))] + [const(a) for a in tabs],
        out_specs=[out, out, out],
        out_shape=[jax.ShapeDtypeStruct((rows, N_SEL), jnp.int32), jax.ShapeDtypeStruct((rows, N_SEL), jnp.int32),
                   jax.ShapeDtypeStruct((rows, N_SEL), F32)],
        scratch_shapes=[pltpu.VMEM((2, N_KEYS, tt), F32), pltpu.VMEM((2, PEER_TOPK, tt), F32), sel_scr, sel_scr, sel_scr],
        compiler_params=_cparams(1), name="peer_topk")(scores_t, *tabs)


def _peer_body(h_ref, ia_ref, ib_ref, gt_ref, ian_ref, ibn_ref, gtn_ref, x1_ref, fg_ref, eu_ref, ev_ref, o_ref,
               gate_s, acc_s, *, tm, n_a):
    i = pl.program_id(0)
    e = pl.program_id(1)
    n_steps = N_KEYS // n_a
    per_step = tm // n_steps
    slot = i % 2
    key_rows = lax.broadcasted_iota(jnp.int32, (N_KEYS, N_SEL), 0)

    def build(refs, dst_slot, t):
        ia, ib, gt = refs
        pa = jnp.where(key_rows == ia[pl.ds(t, 1), :], gt[pl.ds(t, 1), :], 0.0).astype(BF16)
        qb = jnp.where(key_rows == ib[pl.ds(t, 1), :], 1.0, 0.0).astype(BF16)
        gate_s[dst_slot, pl.ds(pl.multiple_of(t * GATE_PITCH, SUBLANES), N_KEYS), :] = _dot_nt(pa, qb)

    @pl.when(e == 0)
    def _():
        acc_s[...] = jnp.zeros((tm, D_MODEL), F32)

    @pl.when((i == 0) & (e == 0))
    def _():
        def body(t, c):
            build((ia_ref, ib_ref, gt_ref), 0, t)
            return c
        lax.fori_loop(0, tm, body, 0, unroll=GATE_BUILD_UNROLL)

    pre = _dot(h_ref[...], eu_ref[...])
    acts = []
    for ai in range(n_a):
        x = pre[:, ai * N_KEYS:(ai + 1) * N_KEYS]
        gate = gate_s[slot, pl.ds(e * n_a + ai, tm, stride=GATE_PITCH), :]
        acts.append((0.5 * x * (1.0 + lax.erf(x * np.float32(np.sqrt(0.5)))) * gate).astype(BF16))
    acc_s[...] += _dot(jnp.concatenate(acts, axis=1), ev_ref[...])

    for j in range(per_step):
        build((ian_ref, ibn_ref, gtn_ref), 1 - slot, e * per_step + j)

    @pl.when(e == pl.num_programs(1) - 1)
    def _():
        o_ref[...] = _rms(x1_ref[...] + acc_s[...], fg_ref[...])


def _peer(h2, ia, ib, gt, x1, fg, eu_t, ev, *, tm, n_a):
    rows = h2.shape[0]
    eb = n_a * N_KEYS
    n_tiles = rows // tm
    tok = lambda w: pl.BlockSpec((tm, w), lambda i, e: (i, 0))
    nxt = pl.BlockSpec((tm, N_SEL), lambda i, e: (jnp.minimum(i + 1, n_tiles - 1), 0))
    return pl.pallas_call(
        functools.partial(_peer_body, tm=tm, n_a=n_a),
        grid=(n_tiles, N_EXPERTS // eb),
        in_specs=[tok(D_MODEL), tok(N_SEL), tok(N_SEL), tok(N_SEL), nxt, nxt, nxt, tok(D_MODEL),
                  pl.BlockSpec((1, D_MODEL), lambda i, e: (0, 0)),
                  pl.BlockSpec((None, D_MODEL, eb), lambda i, e: (e, 0, 0)),
                  pl.BlockSpec((eb, D_MODEL), lambda i, e: (e, 0))],
        out_specs=tok(D_MODEL),
        out_shape=jax.ShapeDtypeStruct((rows, D_MODEL), F32),
        scratch_shapes=[pltpu.VMEM((2, tm * GATE_PITCH, N_KEYS), F32), pltpu.VMEM((tm, D_MODEL), F32)],
        compiler_params=_cparams(2), name="peer_experts")(h2, ia, ib, gt, ia, ib, gt, x1, fg, eu_t, ev)


def _rope_tables(pos):
    half = ROT_DIM // 2
    inv_freq = ROPE_THETA ** (-jnp.arange(0, ROT_DIM, 2, dtype=F32) / ROT_DIM)
    ang = pos.astype(F32)[:, None] * inv_freq[None, :]
    cos, sin = jnp.cos(ang), jnp.sin(ang)
    n = pos.shape[0]
    ones = jnp.ones((n, HEAD_DIM - ROT_DIM), F32)
    zeros = jnp.zeros((n, HEAD_DIM - ROT_DIM), F32)
    zh = jnp.zeros((n, half), F32)
    c = jnp.concatenate([cos, cos, ones], axis=1)
    s_up = jnp.concatenate([-sin, zh, zeros], axis=1)
    s_dn = jnp.concatenate([zh, sin, zeros], axis=1)
    rep = LANES // HEAD_DIM
    return tuple(jnp.tile(t, (1, rep)) for t in (c, s_up, s_dn))


def _block_ones(n, blk):
    i = np.arange(n) // blk
    return jnp.asarray((i[:, None] == i[None, :]).astype(np.float32), dtype=BF16)


def _hi_lo(w):
    hi = w.astype(BF16)
    return hi, (w - hi.astype(F32)).astype(BF16)


def kernel(x_prompt, x_sample, cache_k_win, cache_v_win, state_wkv, state_shift, meta_tokens, norm1_g, w_in, mu_shift, w0, w_lora_w2, a0, w_lora_a2, w_lora_g2, k_k, k_a, r_k, lnx_w, lnx_b, attn_sinks, w_out, norm2_g, w_query, sub_keys, expert_u, expert_v, final_norm_g):
    batch, seq = x_prompt.shape[0], x_prompt.shape[1]
    dec_batch, dec_seq = x_sample.shape[0], x_sample.shape[1]
    l_real = N_META + seq
    front = (-l_real) % ATT_BLOCK
    l_pad = l_real + front
    head_rows = front + N_META
    assert head_rows % ATT_BLOCK == 0 and dec_seq <= SUBLANES and cache_k_win.shape[2] == WINDOW

    hpg = ATTN_HEADS // KV_HEADS
    qperm = np.concatenate([(s + hpg * g) * HEAD_DIM + np.arange(HEAD_DIM) for s in range(hpg) for g in range(KV_HEADS)])
    win = w_in[0]
    win = jnp.concatenate([win[:, :N_RW_COLS], win[:, N_RW_COLS + qperm], win[:, N_RW_COLS + ATTN_WIDTH:]], axis=1).astype(BF16)
    wout = w_out[0]
    wout = jnp.concatenate([wout[:RWKV_WIDTH], wout[RWKV_WIDTH + qperm]], axis=0).astype(BF16)
    sinks = attn_sinks[0]

    row = lambda a: a.reshape(1, -1)
    zpad = lambda w, top: jnp.concatenate([w, jnp.zeros_like(w)] if top else [jnp.zeros_like(w), w], axis=0)
    bd64 = _block_ones(RWKV_WIDTH, HEAD_DIM)
    prep_params = (row(mu_shift[0]), row(w0[0]), row(a0[0]), row(k_k[0]), row(k_a[0]), row(r_k[0]),
                   *_hi_lo(zpad(w_lora_w2[0], True)), *_hi_lo(zpad(w_lora_a2[0], False)), *_hi_lo(w_lora_g2[0]), bd64)
    mix_params = (row(lnx_w[0]), row(lnx_b[0]), bd64, wout, row(norm2_g[0]), w_query[0].astype(BF16), *_hi_lo(sub_keys[0]))
    n_a = 8
    eu_t = expert_u[0].astype(BF16).reshape(N_EXPERTS // (n_a * N_KEYS), n_a * N_KEYS, D_MODEL).transpose(0, 2, 1)
    ev = expert_v[0].astype(BF16)
    g1, fg = row(norm1_g[0]), row(final_norm_g)

    def ffn(x1, h2, scores_t):
        ia, ib, gt = _topk(scores_t, tt=TOPK_TOKEN_TILE)
        return _peer(h2, ia, ib, gt, x1, fg, eu_t, ev, tm=min(PEER_TOKEN_TILE, x1.shape[0]), n_a=n_a)

    xp = jnp.concatenate([jnp.zeros((batch, front, D_MODEL), F32),
                          jnp.broadcast_to(meta_tokens[None], (batch, N_META, D_MODEL)), x_prompt], axis=1)
    proj = _inproj(xp.reshape(batch * l_pad, D_MODEL), g1, win, norm=True, tm=256)
    tabs = _rope_tables(jnp.arange(l_pad) - front)
    r, dec, kx, vx, av, bv, gg, bon, q, ka, va = _prep(proj, proj, tabs, prep_params, seq_rows=l_pad, tm=384, sample=False)
    b3 = lambda a: a.reshape(batch, l_pad, -1)
    y_rw, s_p = _scan(b3(r), b3(dec), b3(kx), b3(vx), b3(av), b3(bv),
                      jnp.zeros((batch * RWKV_HEADS, HEAD_DIM, HEAD_DIM), F32), nb=batch, tc=64)
    y_at = _attn_prompt(sinks, b3(q), b3(ka), b3(va), front=front)
    x1, h2, scores = _mix(xp, y_rw, b3(gg), b3(bon), y_at, mix_params, tm=ATT_BLOCK, skip=head_rows // ATT_BLOCK)
    y_prompt = ffn(x1, h2, scores).reshape(batch, seq, D_MODEL)
    keep = min(WINDOW, l_real)
    k_win_p = b3(ka)[:, l_pad - keep:].reshape(1, batch, keep, KV_HEADS, HEAD_DIM)
    v_win_p = b3(va)[:, l_pad - keep:].reshape(1, batch, keep, KV_HEADS, HEAD_DIM)
    wkv_p = s_p.reshape(1, batch, RWKV_HEADS, HEAD_DIM, HEAD_DIM)

    n_s = dec_batch * dec_seq
    xs = x_sample.reshape(n_s, D_MODEL)
    proj_s = _inproj(xs, g1, win, norm=True, tm=256)
    prev_proj = _inproj(state_shift[0], g1, win, norm=False, tm=dec_batch)
    prev_rows = jnp.pad(prev_proj[:, None, :], ((0, 0), (0, dec_seq - 1), (0, 0))).reshape(n_s, N_IN_COLS)
    pos_s = jnp.tile(PAST_LEN + jnp.arange(dec_seq), dec_batch)
    r, dec, kx, vx, av, bv, gg, bon, q, ka, va = _prep(proj_s, prev_rows, _rope_tables(pos_s), prep_params,
                                                       seq_rows=dec_seq, tm=n_s, sample=True)
    s3 = lambda a: a.reshape(dec_batch, dec_seq, -1)
    y_rw, s_s = _scan(s3(r), s3(dec), s3(kx), s3(vx), s3(av), s3(bv),
                      state_wkv[0].reshape(dec_batch * RWKV_HEADS, HEAD_DIM, HEAD_DIM), nb=4, tc=dec_seq)
    k_buf = cache_k_win[0].reshape(dec_batch, WINDOW, KV_WIDTH)
    v_buf = cache_v_win[0].reshape(dec_batch, WINDOW, KV_WIDTH)
    pad8 = lambda a: jnp.pad(a, ((0, 0), (0, SUBLANES - dec_seq), (0, 0)))
    y_at = _attn_sample(sinks, s3(q), pad8(s3(ka)), pad8(s3(va)), k_buf, v_buf, bb=8)
    one = lambda a: a.reshape(1, n_s, -1)
    x1, h2, scores = _mix(one(xs), one(y_rw), one(gg), one(bon), one(y_at), mix_params, tm=ATT_BLOCK, skip=0)
    y_sample = ffn(x1, h2, scores).reshape(dec_batch, dec_seq, D_MODEL)
    k_win_s = jnp.concatenate([k_buf, s3(ka)], axis=1)[:, -WINDOW:].reshape(1, dec_batch, WINDOW, KV_HEADS, HEAD_DIM)
    v_win_s = jnp.concatenate([v_buf, s3(va)], axis=1)[:, -WINDOW:].reshape(1, dec_batch, WINDOW, KV_HEADS, HEAD_DIM)
    wkv_s = s_s.reshape(1, dec_batch, RWKV_HEADS, HEAD_DIM, HEAD_DIM)

    last = jnp.concatenate([x_prompt[:, -1], x_sample[:, -1]], axis=0)
    pad_rows = (-last.shape[0]) % SUBLANES
    shift = _rmsnorm_rows(jnp.pad(last, ((0, pad_rows), (0, 0))), g1)
    shift_p = shift[:batch].reshape(1, batch, D_MODEL)
    shift_s = shift[batch:batch + dec_batch].reshape(1, dec_batch, D_MODEL)

    return (y_prompt, y_sample, k_win_p, v_win_p, wkv_p, shift_p, k_win_s, v_win_s, wkv_s, shift_s)
```

```python
import functools

import numpy as np
import jax
import jax.numpy as jnp
from jax import lax
from jax.experimental import pallas as pl
from jax.experimental.pallas import tpu as pltpu

F32 = jnp.float32
BF16 = jnp.bfloat16

D_MODEL = 1024
HEAD_DIM = 64
RWKV_WIDTH = 512
RWKV_HEADS = 8
ATTN_WIDTH = 512
ATTN_HEADS = 8
KV_HEADS = 2
KV_WIDTH = KV_HEADS * HEAD_DIM
DECAY_LORA = 64
AAA_LORA = 64
GATE_LORA = 128
N_RW_COLS = 3 * RWKV_WIDTH + DECAY_LORA + AAA_LORA + GATE_LORA
N_IN_COLS = N_RW_COLS + ATTN_WIDTH + 2 * KV_WIDTH
LNX_EPS = 64e-5
WINDOW = 128
ATT_BLOCK = 128
ATTN_SCALE = HEAD_DIM ** -0.5
ROPE_THETA = 500000.0
ROT_DIM = HEAD_DIM // 4
N_META = 16
N_KEYS = 128
N_EXPERTS = N_KEYS * N_KEYS
PEER_HEADS = 8
PEER_TOPK = 16
N_SEL = PEER_HEADS * PEER_TOPK
D_HALF = 128
NORM_EPS = 1e-5
NEG_INF = -1e30
PAST_LEN = 8192

LANES = 128
SUBLANES = 8
GATE_PITCH = N_KEYS + SUBLANES
VMEM_LIMIT = 56 * 1024 * 1024
GATE_BUILD_UNROLL = 16
SCAN_GROUPS = 4
SCAN_UNROLL = 4
TOPK_TOKEN_TILE = 2 * LANES
PEER_TOKEN_TILE = 256


def _cparams(n_axes):
    return pltpu.CompilerParams(dimension_semantics=("arbitrary",) * n_axes, vmem_limit_bytes=VMEM_LIMIT)


def _split2(x):
    hi = x.astype(BF16)
    lo = (x - hi.astype(F32)).astype(BF16)
    return hi, lo


def _split3(x):
    hi = x.astype(BF16)
    r1 = x - hi.astype(F32)
    mid = r1.astype(BF16)
    lo = (r1 - mid.astype(F32)).astype(BF16)
    return hi, mid, lo


def _dot(a, b):
    return jnp.dot(a, b, preferred_element_type=F32)


def _dot_nt(a, b):
    return lax.dot_general(a, b, (((1,), (1,)), ((), ())), preferred_element_type=F32)


def _dot_exact01(x, m01):
    hi, mid, lo = _split3(x)
    return _dot(hi, m01) + _dot(mid, m01) + _dot(lo, m01)


def _dot3(x, wh, wl):
    xh, xl = _split2(x)
    return _dot(xh, wh) + (_dot(xl, wh) + _dot(xh, wl))


def _dot3_nt(x, y):
    xh, xl = _split2(x)
    yh, yl = _split2(y)
    return _dot_nt(xh, yh) + (_dot_nt(xl, yh) + _dot_nt(xh, yl))


def _rms(x, g):
    return x * lax.rsqrt(jnp.mean(x * x, axis=-1, keepdims=True) + NORM_EPS) * g


def _inproj_body(x_ref, g_ref, w_ref, o_ref, *, norm):
    x = x_ref[...]
    if norm:
        x = _rms(x, g_ref[...])
    o_ref[...] = _dot(x.astype(BF16), w_ref[...])


def _inproj(x, g, w_bf, *, norm, tm):
    rows, ncol = x.shape[0], w_bf.shape[1]
    return pl.pallas_call(
        functools.partial(_inproj_body, norm=norm),
        grid=(rows // tm,),
        in_specs=[pl.BlockSpec((tm, D_MODEL), lambda i: (i, 0)),
                  pl.BlockSpec((1, D_MODEL), lambda i: (0, 0)),
                  pl.BlockSpec((D_MODEL, ncol), lambda i: (0, 0))],
        out_specs=pl.BlockSpec((tm, ncol), lambda i: (i, 0)),
        out_shape=jax.ShapeDtypeStruct((rows, ncol), F32),
        compiler_params=_cparams(1), name="inproj")(x, g, w_bf)


def _rmsnorm_body(x_ref, g_ref, o_ref):
    o_ref[...] = _rms(x_ref[...], g_ref[...])


def _rmsnorm_rows(x, g):
    rows = x.shape[0]
    return pl.pallas_call(
        _rmsnorm_body, grid=(1,),
        in_specs=[pl.BlockSpec((rows, D_MODEL), lambda i: (0, 0)), pl.BlockSpec((1, D_MODEL), lambda i: (0, 0))],
        out_specs=pl.BlockSpec((rows, D_MODEL), lambda i: (0, 0)),
        out_shape=jax.ShapeDtypeStruct((rows, D_MODEL), F32),
        compiler_params=_cparams(1), name="rmsnorm_rows")(x, g)


def _rope(x, c, s_up, s_dn):
    up = pltpu.roll(x, LANES - ROT_DIM // 2, 1)
    dn = pltpu.roll(x, ROT_DIM // 2, 1)
    return x * c + up * s_up + dn * s_dn


def _prep_body(*refs, seq_rows, tile_rows, sample):
    if sample:
        p_ref, prev_ref = refs[0], refs[1]
    else:
        p_ref, prev8_ref = refs[0], refs[1]
    (cos_ref, sup_ref, sdn_ref, mu_ref, w0_ref, a0_ref, kk_ref, ka_ref, rk_ref,
     w2h_ref, w2l_ref, a2h_ref, a2l_ref, g2h_ref, g2l_ref, bd_ref) = refs[2:18]
    r_o, dec_o, k_o, v_o, av_o, bv_o, g_o, bon_o, q_o, ka_o, va_o = refs[18:]

    prw = p_ref[:, :N_RW_COLS]
    rolled = pltpu.roll(prw, 1, 0)
    row = lax.broadcasted_iota(jnp.int32, (tile_rows, 1), 0)
    if sample:
        pprev = jnp.where(row % seq_rows == 0, prev_ref[:, :N_RW_COLS], rolled)
    else:
        tiles_per_seq = seq_rows // tile_rows
        at_start = (pl.program_id(0) % tiles_per_seq) == 0
        first = jnp.where(at_start, 0.0, prev8_ref[SUBLANES - 1:SUBLANES, :N_RW_COLS])
        pprev = jnp.where(row == 0, first, rolled)
    m = prw + (pprev - prw) * mu_ref[...]

    c = RWKV_WIDTH
    xr, xk, xv = m[:, :c], m[:, c:2 * c], m[:, 2 * c:3 * c]
    xwa = m[:, 3 * c:3 * c + LANES]
    xg = m[:, 3 * c + LANES:3 * c + 2 * LANES]

    lw = w0_ref[...] + _dot3(jnp.tanh(xwa), w2h_ref[...], w2l_ref[...])
    z = -lw
    w = -(jnp.maximum(z, 0.0) + jnp.log1p(jnp.exp(-jnp.abs(z)))) - 0.5
    dec_o[...] = jnp.exp(-jnp.exp(w))
    a = jax.nn.sigmoid(a0_ref[...] + _dot3(xwa, a2h_ref[...], a2l_ref[...]))
    g_o[...] = _dot3(jax.nn.sigmoid(xg), g2h_ref[...], g2l_ref[...])

    bd = bd_ref[...]
    kk = xk * kk_ref[...]
    kk = kk / jnp.maximum(jnp.sqrt(_dot_exact01(kk * kk, bd)), 1e-12)
    kmod = xk * (1.0 + (a - 1.0) * ka_ref[...])
    r_o[...] = xr
    k_o[...] = kmod
    v_o[...] = xv
    av_o[...] = -kk
    bv_o[...] = kk * a
    bon_o[...] = _dot_exact01(xr * kmod * rk_ref[...], bd) * xv

    cs, su, sd = cos_ref[...], sup_ref[...], sdn_ref[...]
    o = N_RW_COLS
    for s in range(ATTN_WIDTH // LANES):
        q_o[:, s * LANES:(s + 1) * LANES] = _rope(p_ref[:, o + s * LANES:o + (s + 1) * LANES], cs, su, sd) * ATTN_SCALE
    o += ATTN_WIDTH
    ka_o[...] = _rope(p_ref[:, o:o + KV_WIDTH], cs, su, sd)
    va_o[...] = p_ref[:, o + KV_WIDTH:o + 2 * KV_WIDTH]


def _prep(p, prev, tabs, params, *, seq_rows, tm, sample):
    rows = p.shape[0]
    n_tiles = rows // tm
    row_spec = lambda w: pl.BlockSpec((tm, w), lambda i: (i, 0))
    const = lambda a: pl.BlockSpec(a.shape, lambda i: (0,) * a.ndim)
    if sample:
        prev_spec = row_spec(N_IN_COLS)
        tab_spec = row_spec(LANES)
    else:
        blocks = tm // SUBLANES
        prev_spec = pl.BlockSpec((SUBLANES, N_IN_COLS), lambda i: (jnp.maximum(i * blocks - 1, 0), 0))
        tiles_per_seq = seq_rows // tm
        tab_spec = pl.BlockSpec((tm, LANES), lambda i: (i % tiles_per_seq, 0))
    in_specs = [row_spec(N_IN_COLS), prev_spec, tab_spec, tab_spec, tab_spec] + [const(a) for a in params]
    widths = [RWKV_WIDTH] * 8 + [ATTN_WIDTH, KV_WIDTH, KV_WIDTH]
    return pl.pallas_call(
        functools.partial(_prep_body, seq_rows=seq_rows, tile_rows=tm, sample=sample),
        grid=(n_tiles,), in_specs=in_specs,
        out_specs=[row_spec(w) for w in widths],
        out_shape=[jax.ShapeDtypeStruct((rows, w), F32) for w in widths],
        compiler_params=_cparams(1), name="prep")(p, prev, *tabs, *params)


def _scan_body(r_ref, w_ref, k_ref, v_ref, a_ref, b_ref, s0_ref, bd_ref, eye_ref, esh_ref, y_ref, so_ref,
               st_s, *row_s, nb, tc):
    ci = pl.program_id(1)
    slabs = RWKV_WIDTH // LANES
    n_tiles = nb * slabs
    n_pairs = n_tiles // 2
    heads_per_tile = LANES // HEAD_DIM
    r_s, w_s, k_s, vh_s, a_s, b_s, vl_s, y_s, p2_s, sa_s, vc_s = row_s

    def tile_states(p):
        return [((p // slabs) * RWKV_HEADS + heads_per_tile * (p % slabs) + m, m * HEAD_DIM) for m in range(heads_per_tile)]

    @pl.when(ci == 0)
    def _():
        for p in range(n_tiles):
            for s, off in tile_states(p):
                st_s[p, :, off:off + HEAD_DIM] = s0_ref[s]

        p2_s[...] = jnp.zeros(p2_s.shape, BF16)

    bd = bd_ref[...]
    eye = eye_ref[...]
    eye_bf = eye.astype(BF16)
    far_bf = esh_ref[...].astype(BF16)

    half_head = HEAD_DIM // 2
    first_half = lax.broadcasted_iota(jnp.int32, (tc, LANES), 1) % HEAD_DIM < half_head
    for p in range(n_tiles):
        sl = (p // slabs, slice(None), slice((p % slabs) * LANES, (p % slabs + 1) * LANES))
        for src, dst in ((r_ref, r_s), (w_ref, w_s), (k_ref, k_s), (a_ref, a_s), (b_ref, b_s)):
            dst[p] = src[sl]
        v = v_ref[sl]
        hi = v.astype(BF16).astype(F32)
        lo = (v - hi).astype(BF16).astype(F32)
        vh_s[p] = hi
        vl_s[p] = jnp.where(first_half, pltpu.roll(lo, LANES - half_head, 1), pltpu.roll(lo, half_head, 1))

    def row(ref, t, p):
        return ref[p, pl.ds(t, 1), :]

    n_groups = min(SCAN_GROUPS, n_pairs)
    group_pairs = n_pairs // n_groups
    group_rows = group_pairs * HEAD_DIM

    def side_by_side(lo_tiles, hi_tiles):
        return jnp.concatenate([jnp.concatenate([a, b], axis=1) for a, b in zip(lo_tiles, hi_tiles)], axis=0)

    def v_diag(t, p):
        return eye_bf * row(vh_s, t, p).astype(BF16) + far_bf * row(vl_s, t, p).astype(BF16)

    def lookahead(g, states, t):
        pairs = range(g * group_pairs, (g + 1) * group_pairs)
        sa_in = side_by_side([(states[pp] * row(a_s, t, pp)).astype(BF16) for pp in pairs],
                             [(states[pp + n_pairs] * row(a_s, t, pp + n_pairs)).astype(BF16) for pp in pairs])
        vc_in = side_by_side([v_diag(t, pp) for pp in pairs], [v_diag(t, pp + n_pairs) for pp in pairs])
        return _dot(sa_in, bd), _dot(vc_in, bd)

    def emit_y(row_idx, p2_all):
        yc = _dot(p2_all, bd)
        for p in range(n_tiles):
            rows = slice((p % n_pairs) * HEAD_DIM, (p % n_pairs + 1) * HEAD_DIM)
            tile = yc[rows, (p // n_pairs) * LANES:(p // n_pairs + 1) * LANES]
            y_s[p, pl.ds(row_idx, 1), :] = jnp.sum(tile * eye, axis=0, keepdims=True)

    for g in range(n_groups):
        states = {p: st_s[p] for pp in range(g * group_pairs, (g + 1) * group_pairs) for p in (pp, pp + n_pairs)}
        sa0, vc0 = lookahead(g, states, 0)
        sa_s[g * group_rows:(g + 1) * group_rows, :] = sa0
        vc_s[g * group_rows:(g + 1) * group_rows, :] = vc0

    def step(t, st, ahead, p2_all):
        emit_y(t + (SUBLANES - 1), p2_all)
        t_next = jnp.minimum(t + 1, tc - 1)
        new_st, new_ahead, p2 = {}, [], {}
        for g in range(n_groups):
            sa_g, vc_g = ahead[g]
            states = {}
            for pp in range(g * group_pairs, (g + 1) * group_pairs):
                r0 = (pp - g * group_pairs) * HEAD_DIM
                for half, p in enumerate((pp, pp + n_pairs)):
                    lanes = slice(half * LANES, (half + 1) * LANES)
                    new = (st[p] * row(w_s, t, p) + sa_g[r0:r0 + HEAD_DIM, lanes] * row(b_s, t, p)
                           + vc_g[r0:r0 + HEAD_DIM, lanes] * row(k_s, t, p))
                    states[p] = new
                    p2[p] = (new * row(r_s, t, p)).astype(BF16)
            new_st.update(states)
            new_ahead.append(lookahead(g, states, t_next))
        p2_all = side_by_side([p2[pp] for pp in range(n_pairs)], [p2[pp + n_pairs] for pp in range(n_pairs)])
        return new_st, new_ahead, p2_all

    def body(i, carry):
        st = {p: st_s[p] for p in range(n_tiles)}
        ahead = [(sa_s[g * group_rows:(g + 1) * group_rows, :], vc_s[g * group_rows:(g + 1) * group_rows, :])
                 for g in range(n_groups)]
        p2_all = p2_s[...]
        for k in range(SCAN_UNROLL):
            st, ahead, p2_all = step(i * SCAN_UNROLL + k, st, ahead, p2_all)
        for p in range(n_tiles):
            st_s[p] = st[p]
        for g, (sa_n, vc_n) in enumerate(ahead):
            sa_s[g * group_rows:(g + 1) * group_rows, :] = sa_n
            vc_s[g * group_rows:(g + 1) * group_rows, :] = vc_n
        p2_s[...] = p2_all
        return carry

    lax.fori_loop(0, tc // SCAN_UNROLL, body, 0)
    emit_y(tc + (SUBLANES - 1), p2_s[...])
    for p in range(n_tiles):
        y_ref[p // slabs, :, (p % slabs) * LANES:(p % slabs + 1) * LANES] = y_s[p, SUBLANES:, :]

    @pl.when(ci == pl.num_programs(1) - 1)
    def _():
        for p in range(n_tiles):
            for s, off in tile_states(p):
                so_ref[s] = st_s[p, :, off:off + HEAD_DIM]


def _scan(r, w, k, v, a, b, s0, *, nb, tc):
    batch, t_len = r.shape[0], r.shape[1]
    ns = nb * RWKV_HEADS
    n_tiles = nb * RWKV_WIDTH // LANES
    bd = _block_ones(2 * LANES, HEAD_DIM)
    lane_in_head = np.arange(LANES)[None, :] % HEAD_DIM
    value_row = np.arange(HEAD_DIM)[:, None]
    eye = jnp.asarray(lane_in_head == value_row, dtype=F32)
    eye_far = jnp.asarray(lane_in_head == (value_row + HEAD_DIM // 2) % HEAD_DIM, dtype=F32)
    seq_spec = pl.BlockSpec((nb, tc, RWKV_WIDTH), lambda g, c: (g, c, 0))
    st_spec = pl.BlockSpec((ns, HEAD_DIM, HEAD_DIM), lambda g, c: (g, 0, 0))
    const = lambda x: pl.BlockSpec(x.shape, lambda g, c: (0,) * x.ndim)
    chunk_scr = pltpu.VMEM((n_tiles, tc, LANES), F32)
    return pl.pallas_call(
        functools.partial(_scan_body, nb=nb, tc=tc),
        grid=(batch // nb, t_len // tc),
        in_specs=[seq_spec] * 6 + [st_spec, const(bd), const(eye), const(eye_far)],
        out_specs=[seq_spec, st_spec],
        out_shape=[jax.ShapeDtypeStruct(r.shape, F32), jax.ShapeDtypeStruct(s0.shape, F32)],
        scratch_shapes=[pltpu.VMEM((n_tiles, HEAD_DIM, LANES), F32)] + [chunk_scr] * 7
        + [pltpu.VMEM((n_tiles, tc + SUBLANES, LANES), F32), pltpu.VMEM((n_tiles // 2 * HEAD_DIM, 2 * LANES), BF16)]
        + [pltpu.VMEM((n_tiles // 2 * HEAD_DIM, 2 * LANES), F32)] * 2,
        compiler_params=_cparams(2), name="wkv_scan")(r, w, k, v, a, b, s0, bd, eye, eye_far)


def _sink_softmax(sc, valid, sink):
    sm = jnp.where(valid, sc, NEG_INF)
    mx = jnp.maximum(jnp.max(sm, axis=-1, keepdims=True), sink)
    p = jnp.exp(sm - mx)
    return p / (jnp.sum(p, axis=-1, keepdims=True) + jnp.exp(sink - mx))


def _attn_prompt_body(sink_ref, q_ref, kp_ref, kc_ref, vp_ref, vc_ref, o_ref, *, front):
    n = pl.program_id(1)
    keys = jnp.concatenate([kp_ref[...], kc_ref[...]], axis=0)
    vals = jnp.concatenate([vp_ref[...], vc_ref[...]], axis=0)
    shp = (ATT_BLOCK, 2 * ATT_BLOCK)
    qpos = n * ATT_BLOCK - front + lax.broadcasted_iota(jnp.int32, shp, 0)
    kpos = (n - 1) * ATT_BLOCK - front + lax.broadcasted_iota(jnp.int32, shp, 1)
    dlt = qpos - kpos
    valid = (kpos >= 0) & (dlt >= 0) & (dlt < WINDOW)
    half = lax.broadcasted_iota(jnp.int32, (ATT_BLOCK, LANES), 1) // HEAD_DIM
    for s in range(ATTN_WIDTH // LANES):
        q = q_ref[:, s * LANES:(s + 1) * LANES]
        acc = jnp.zeros((ATT_BLOCK, LANES), F32)
        for g in range(KV_HEADS):
            sc = _dot3_nt(jnp.where(half == g, q, 0.0), keys)
            p = _sink_softmax(sc, valid, sink_ref[s + g * (ATTN_HEADS // KV_HEADS)])
            ph, plo = _split2(p)
            vh, vl = _split2(vals)
            o = _dot(ph, vh) + (_dot(plo, vh) + _dot(ph, vl))
            acc = acc + jnp.where(half == g, o, 0.0)
        o_ref[:, s * LANES:(s + 1) * LANES] = acc


def _attn_prompt(sinks, q, k, v, *, front):
    batch, t_len = q.shape[0], q.shape[1]
    blk = lambda w, f: pl.BlockSpec((None, ATT_BLOCK, w), f)
    cur = lambda b, n: (b, n, 0)
    prv = lambda b, n: (b, jnp.maximum(n - 1, 0), 0)
    return pl.pallas_call(
        functools.partial(_attn_prompt_body, front=front),
        grid=(batch, t_len // ATT_BLOCK),
        in_specs=[pl.BlockSpec(memory_space=pltpu.SMEM), blk(ATTN_WIDTH, cur),
                  blk(KV_WIDTH, prv), blk(KV_WIDTH, cur), blk(KV_WIDTH, prv), blk(KV_WIDTH, cur)],
        out_specs=blk(ATTN_WIDTH, cur),
        out_shape=jax.ShapeDtypeStruct(q.shape, F32),
        compiler_params=_cparams(2), name="attn_prompt")(sinks, q, k, k, v, v)


def _attn_sample_body(sink_ref, q_ref, kn_ref, vn_ref, kb_ref, vb_ref, o_ref, *, bb, n_new):
    keys = jnp.concatenate([kb_ref[...], kn_ref[...]], axis=1)
    vals = jnp.concatenate([vb_ref[...], vn_ref[...]], axis=1)
    n_keys = WINDOW + SUBLANES
    shp = (n_new, n_keys)
    tq = lax.broadcasted_iota(jnp.int32, shp, 0)
    sk = lax.broadcasted_iota(jnp.int32, shp, 1)
    dlt = tq - (sk - WINDOW)
    valid = ((dlt >= 0) & (dlt < WINDOW))[None]
    half = lax.broadcasted_iota(jnp.int32, (bb, n_new, LANES), 2) // HEAD_DIM
    bdot = lambda x, y, cd: lax.dot_general(x, y, ((cd, ((0,), (0,)))), preferred_element_type=F32)
    kh, kl = _split2(keys)
    vh, vl = _split2(vals)
    for s in range(ATTN_WIDTH // LANES):
        q = q_ref[:, :, s * LANES:(s + 1) * LANES]
        acc = jnp.zeros((bb, n_new, LANES), F32)
        for g in range(KV_HEADS):
            qh, ql = _split2(jnp.where(half == g, q, 0.0))
            nt = ((2,), (2,))
            sc = bdot(qh, kh, nt) + (bdot(ql, kh, nt) + bdot(qh, kl, nt))
            p = _sink_softmax(sc, valid, sink_ref[s + g * (ATTN_HEADS // KV_HEADS)])
            ph, plo = _split2(p)
            nn = ((2,), (1,))
            o = bdot(ph, vh, nn) + (bdot(plo, vh, nn) + bdot(ph, vl, nn))
            acc = acc + jnp.where(half == g, o, 0.0)
        o_ref[:, :, s * LANES:(s + 1) * LANES] = acc


def _attn_sample(sinks, q, k_new, v_new, k_buf, v_buf, *, bb):
    batch, n_new = q.shape[0], q.shape[1]
    spec = lambda a: pl.BlockSpec((bb,) + a.shape[1:], lambda i: (i, 0, 0))
    return pl.pallas_call(
        functools.partial(_attn_sample_body, bb=bb, n_new=n_new),
        grid=(batch // bb,),
        in_specs=[pl.BlockSpec(memory_space=pltpu.SMEM)] + [spec(a) for a in (q, k_new, v_new, k_buf, v_buf)],
        out_specs=spec(q),
        out_shape=jax.ShapeDtypeStruct(q.shape, F32),
        compiler_params=_cparams(1), name="attn_sample")(sinks, q, k_new, v_new, k_buf, v_buf)


def _mix_body(x_ref, y_ref, g_ref, bon_ref, ya_ref, lw_ref, lb_ref, bd_ref, wo_ref, n2_ref, wq_ref,
              skh_ref, skl_ref, x1_o, h2_o, sc_o):
    bd = bd_ref[...]
    y = y_ref[...]
    inv = 1.0 / HEAD_DIM
    mean = _dot_exact01(y, bd) * inv
    yc = y - mean
    var = _dot_exact01(yc * yc, bd) * inv
    yn = yc * lax.rsqrt(var + LNX_EPS) * lw_ref[...] + lb_ref[...]
    yrw = (yn + bon_ref[...]) * g_ref[...]
    x1 = x_ref[...] + (_dot(yrw.astype(BF16), wo_ref[:RWKV_WIDTH, :]) + _dot(ya_ref[...].astype(BF16), wo_ref[RWKV_WIDTH:, :]))
    x1_o[...] = x1
    h2 = _rms(x1, n2_ref[...]).astype(BF16)
    h2_o[...] = h2
    qp = _dot(h2, wq_ref[...])
    for hc in range(2 * PEER_HEADS):
        c = hc % 2
        qh, ql = _split2(qp[:, hc * D_HALF:(hc + 1) * D_HALF])
        sc_o[hc] = _dot_nt(skh_ref[c], qh) + (_dot_nt(skh_ref[c], ql) + _dot_nt(skl_ref[c], qh))


def _mix(x, y, g, bon, ya, params, *, tm, skip):
    batch, t_len = x.shape[0], x.shape[1]
    n_out = t_len // tm - skip
    rows = batch * n_out * tm
    src = lambda w: pl.BlockSpec((None, tm, w), lambda b, j: (b, j + skip, 0))
    dst = lambda w: pl.BlockSpec((tm, w), lambda b, j: (b * n_out + j, 0))
    const = lambda a: pl.BlockSpec(a.shape, lambda b, j: (0,) * a.ndim)
    return pl.pallas_call(
        _mix_body, grid=(batch, n_out),
        in_specs=[src(D_MODEL), src(RWKV_WIDTH), src(RWKV_WIDTH), src(RWKV_WIDTH), src(ATTN_WIDTH)]
        + [const(a) for a in params],
        out_specs=[dst(D_MODEL), dst(D_MODEL),
                   pl.BlockSpec((2 * PEER_HEADS, N_KEYS, tm), lambda b, j: (0, 0, b * n_out + j))],
        out_shape=[jax.ShapeDtypeStruct((rows, D_MODEL), F32),
                   jax.ShapeDtypeStruct((rows, D_MODEL), BF16),
                   jax.ShapeDtypeStruct((2 * PEER_HEADS, N_KEYS, rows), F32)],
        compiler_params=_cparams(2), name="mix")(x, y, g, bon, ya, *params)


def _pair_tables(tt):
    pairs = [(k1, k2) for k1 in range(PEER_TOPK) for k2 in range(PEER_TOPK) if (k1 + 1) * (k2 + 1) <= PEER_TOPK]
    n_rows = -(-len(pairs) // SUBLANES) * SUBLANES
    g1 = np.zeros((n_rows, N_KEYS), np.float32)
    g2 = np.zeros((n_rows, N_KEYS), np.float32)
    neg = np.full((n_rows, tt), -np.inf, np.float32)
    flat = np.full((n_rows, tt), float(PEER_TOPK * PEER_TOPK), np.float32)
    for r, (k1, k2) in enumerate(pairs):
        g1[r, k1] = 1.0
        g2[r, k2] = 1.0
        neg[r] = 0.0
        flat[r] = k1 * PEER_TOPK + k2
    return jnp.asarray(g1, dtype=BF16), jnp.asarray(g2, dtype=BF16), jnp.asarray(neg), jnp.asarray(flat)


def _topk_body(s_ref, g1_ref, g2_ref, neg_ref, flat_ref, ia_o, ib_o, gt_o, v_s, i_s, a_s, b_s, t_s, *, tt):
    key = lax.broadcasted_iota(jnp.int32, (N_KEYS, tt), 0).astype(F32)
    rank = lax.broadcasted_iota(jnp.int32, (PEER_TOPK, tt), 0).astype(F32)
    neg, flat = neg_ref[...], flat_ref[...]
    v_s[...] = jnp.zeros((2, N_KEYS, tt), F32)

    def expand(g, v):
        hi, mid, lo = _split3(v)
        return _dot(g, hi) + _dot(g, mid) + _dot(g, lo)

    def head(h, carry):
        for c in range(2):
            x = s_ref[2 * h + c]
            for it in range(PEER_TOPK):
                m = jnp.max(x, axis=0, keepdims=True)
                am = jnp.min(jnp.where(x == m, key, float(N_KEYS)), axis=0, keepdims=True)
                v_s[c, it:it + 1, :] = m
                i_s[c, it:it + 1, :] = am
                x = jnp.where(key == am, -jnp.inf, x)
        cand = expand(g1_ref[...], v_s[0]) + expand(g2_ref[...], v_s[1]) + neg
        i1, i2 = i_s[0], i_s[1]
        base = pl.multiple_of(h * PEER_TOPK, PEER_TOPK)
        for it in range(PEER_TOPK):
            m = jnp.max(cand, axis=0, keepdims=True)
            fl = jnp.min(jnp.where(cand == m, flat, float(PEER_TOPK * PEER_TOPK)), axis=0, keepdims=True)
            k1 = jnp.floor(fl * (1.0 / PEER_TOPK))
            k2 = fl - k1 * PEER_TOPK
            a_s[pl.ds(base + it, 1), :] = jnp.sum(jnp.where(rank == k1, i1, 0.0), axis=0, keepdims=True)
            b_s[pl.ds(base + it, 1), :] = jnp.sum(jnp.where(rank == k2, i2, 0.0), axis=0, keepdims=True)
            t_s[pl.ds(base + it, 1), :] = m
            cand = jnp.where(flat == fl, -jnp.inf, cand)
        top = t_s[pl.ds(base, PEER_TOPK), :]
        e = jnp.exp(top - top[0:1])
        t_s[pl.ds(base, PEER_TOPK), :] = e / jnp.sum(e, axis=0, keepdims=True)
        return carry

    lax.fori_loop(0, PEER_HEADS, head, 0)
    ia_o[...] = a_s[...].T.astype(jnp.int32)
    ib_o[...] = b_s[...].T.astype(jnp.int32)
    gt_o[...] = t_s[...].T


def _topk(scores_t, *, tt):
    rows = scores_t.shape[2]
    tabs = _pair_tables(tt)
    out = pl.BlockSpec((tt, N_SEL), lambda i: (i, 0))
    const = lambda a: pl.BlockSpec(a.shape, lambda i: (0,) * a.ndim)
    sel_scr = pltpu.VMEM((N_SEL, tt), F32)
    return pl.pallas_call(
        functools.partial(_topk_body, tt=tt), grid=(rows // tt,),
        in_specs=[pl.BlockSpec((2 * PEER_HEADS, N_KEYS, tt), lambda i: (0, 0, i))] + [const(a) for a in tabs],
        out_specs=[out, out, out],
        out_shape=[jax.ShapeDtypeStruct((rows, N_SEL), jnp.int32), jax.ShapeDtypeStruct((rows, N_SEL), jnp.int32),
                   jax.ShapeDtypeStruct((rows, N_SEL), F32)],
        scratch_shapes=[pltpu.VMEM((2, N_KEYS, tt), F32), pltpu.VMEM((2, PEER_TOPK, tt), F32), sel_scr, sel_scr, sel_scr],
        compiler_params=_cparams(1), name="peer_topk")(scores_t, *tabs)


def _peer_body(h_ref, ia_ref, ib_ref, gt_ref, ian_ref, ibn_ref, gtn_ref, x1_ref, fg_ref, eu_ref, ev_ref, o_ref,
               gate_s, acc_s, *, tm, n_a):
    i = pl.program_id(0)
    e = pl.program_id(1)
    n_steps = N_KEYS // n_a
    per_step = tm // n_steps
    slot = i % 2
    key_rows = lax.broadcasted_iota(jnp.int32, (N_KEYS, N_SEL), 0)

    def build(refs, dst_slot, t):
        ia, ib, gt = refs
        pa = jnp.where(key_rows == ia[pl.ds(t, 1), :], gt[pl.ds(t, 1), :], 0.0).astype(BF16)
        qb = jnp.where(key_rows == ib[pl.ds(t, 1), :], 1.0, 0.0).astype(BF16)
        gate_s[dst_slot, pl.ds(pl.multiple_of(t * GATE_PITCH, SUBLANES), N_KEYS), :] = _dot_nt(pa, qb)

    @pl.when(e == 0)
    def _():
        acc_s[...] = jnp.zeros((tm, D_MODEL), F32)

    @pl.when((i == 0) & (e == 0))
    def _():
        def body(t, c):
            build((ia_ref, ib_ref, gt_ref), 0, t)
            return c
        lax.fori_loop(0, tm, body, 0, unroll=GATE_BUILD_UNROLL)

    pre = _dot(h_ref[...], eu_ref[...])
    acts = []
    for ai in range(n_a):
        x = pre[:, ai * N_KEYS:(ai + 1) * N_KEYS]
        gate = gate_s[slot, pl.ds(e * n_a + ai, tm, stride=GATE_PITCH), :]
        acts.append((0.5 * x * (1.0 + lax.erf(x * np.float32(np.sqrt(0.5)))) * gate).astype(BF16))
    acc_s[...] += _dot(jnp.concatenate(acts, axis=1), ev_ref[...])

    for j in range(per_step):
        build((ian_ref, ibn_ref, gtn_ref), 1 - slot, e * per_step + j)

    @pl.when(e == pl.num_programs(1) - 1)
    def _():
        o_ref[...] = _rms(x1_ref[...] + acc_s[...], fg_ref[...])


def _peer(h2, ia, ib, gt, x1, fg, eu_t, ev, *, tm, n_a):
    rows = h2.shape[0]
    eb = n_a * N_KEYS
    n_tiles = rows // tm
    tok = lambda w: pl.BlockSpec((tm, w), lambda i, e: (i, 0))
    nxt = pl.BlockSpec((tm, N_SEL), lambda i, e: (jnp.minimum(i + 1, n_tiles - 1), 0))
    return pl.pallas_call(
        functools.partial(_peer_body, tm=tm, n_a=n_a),
        grid=(n_tiles, N_EXPERTS // eb),
        in_specs=[tok(D_MODEL), tok(N_SEL), tok(N_SEL), tok(N_SEL), nxt, nxt, nxt, tok(D_MODEL),
                  pl.BlockSpec((1, D_MODEL), lambda i, e: (0, 0)),
                  pl.BlockSpec((None, D_MODEL, eb), lambda i, e: (e, 0, 0)),
                  pl.BlockSpec((eb, D_MODEL), lambda i, e: (e, 0))],
        out_specs=tok(D_MODEL),
        out_shape=jax.ShapeDtypeStruct((rows, D_MODEL), F32),
        scratch_shapes=[pltpu.VMEM((2, tm * GATE_PITCH, N_KEYS), F32), pltpu.VMEM((tm, D_MODEL), F32)],
        compiler_params=_cparams(2), name="peer_experts")(h2, ia, ib, gt, ia, ib, gt, x1, fg, eu_t, ev)


def _rope_tables(pos):
    half = ROT_DIM // 2
    inv_freq = ROPE_THETA ** (-jnp.arange(0, ROT_DIM, 2, dtype=F32) / ROT_DIM)
    ang = pos.astype(F32)[:, None] * inv_freq[None, :]
    cos, sin = jnp.cos(ang), jnp.sin(ang)
    n = pos.shape[0]
    ones = jnp.ones((n, HEAD_DIM - ROT_DIM), F32)
    zeros = jnp.zeros((n, HEAD_DIM - ROT_DIM), F32)
    zh = jnp.zeros((n, half), F32)
    c = jnp.concatenate([cos, cos, ones], axis=1)
    s_up = jnp.concatenate([-sin, zh, zeros], axis=1)
    s_dn = jnp.concatenate([zh, sin, zeros], axis=1)
    rep = LANES // HEAD_DIM
    return tuple(jnp.tile(t, (1, rep)) for t in (c, s_up, s_dn))


def _block_ones(n, blk):
    i = np.arange(n) // blk
    return jnp.asarray((i[:, None] == i[None, :]).astype(np.float32), dtype=BF16)


def _hi_lo(w):
    hi = w.astype(BF16)
    return hi, (w - hi.astype(F32)).astype(BF16)


def kernel(x_prompt, x_sample, cache_k_win, cache_v_win, state_wkv, state_shift, meta_tokens, norm1_g, w_in, mu_shift, w0, w_lora_w2, a0, w_lora_a2, w_lora_g2, k_k, k_a, r_k, lnx_w, lnx_b, attn_sinks, w_out, norm2_g, w_query, sub_keys, expert_u, expert_v, final_norm_g):
    batch, seq = x_prompt.shape[0], x_prompt.shape[1]
    dec_batch, dec_seq = x_sample.shape[0], x_sample.shape[1]
    l_real = N_META + seq
    front = (-l_real) % ATT_BLOCK
    l_pad = l_real + front
    head_rows = front + N_META
    assert head_rows % ATT_BLOCK == 0 and dec_seq <= SUBLANES and cache_k_win.shape[2] == WINDOW

    hpg = ATTN_HEADS // KV_HEADS
    qperm = np.concatenate([(s + hpg * g) * HEAD_DIM + np.arange(HEAD_DIM) for s in range(hpg) for g in range(KV_HEADS)])
    win = w_in[0]
    win = jnp.concatenate([win[:, :N_RW_COLS], win[:, N_RW_COLS + qperm], win[:, N_RW_COLS + ATTN_WIDTH:]], axis=1).astype(BF16)
    wout = w_out[0]
    wout = jnp.concatenate([wout[:RWKV_WIDTH], wout[RWKV_WIDTH + qperm]], axis=0).astype(BF16)
    sinks = attn_sinks[0]

    row = lambda a: a.reshape(1, -1)
    zpad = lambda w, top: jnp.concatenate([w, jnp.zeros_like(w)] if top else [jnp.zeros_like(w), w], axis=0)
    bd64 = _block_ones(RWKV_WIDTH, HEAD_DIM)
    prep_params = (row(mu_shift[0]), row(w0[0]), row(a0[0]), row(k_k[0]), row(k_a[0]), row(r_k[0]),
                   *_hi_lo(zpad(w_lora_w2[0], True)), *_hi_lo(zpad(w_lora_a2[0], False)), *_hi_lo(w_lora_g2[0]), bd64)
    mix_params = (row(lnx_w[0]), row(lnx_b[0]), bd64, wout, row(norm2_g[0]), w_query[0].astype(BF16), *_hi_lo(sub_keys[0]))
    n_a = 8
    eu_t = expert_u[0].astype(BF16).reshape(N_EXPERTS // (n_a * N_KEYS), n_a * N_KEYS, D_MODEL).transpose(0, 2, 1)
    ev = expert_v[0].astype(BF16)
    g1, fg = row(norm1_g[0]), row(final_norm_g)

    def ffn(x1, h2, scores_t):
        ia, ib, gt = _topk(scores_t, tt=TOPK_TOKEN_TILE)
        return _peer(h2, ia, ib, gt, x1, fg, eu_t, ev, tm=min(PEER_TOKEN_TILE, x1.shape[0]), n_a=n_a)

    xp = jnp.concatenate([jnp.zeros((batch, front, D_MODEL), F32),
                          jnp.broadcast_to(meta_tokens[None], (batch, N_META, D_MODEL)), x_prompt], axis=1)
    proj = _inproj(xp.reshape(batch * l_pad, D_MODEL), g1, win, norm=True, tm=256)
    tabs = _rope_tables(jnp.arange(l_pad) - front)
    r, dec, kx, vx, av, bv, gg, bon, q, ka, va = _prep(proj, proj, tabs, prep_params, seq_rows=l_pad, tm=384, sample=False)
    b3 = lambda a: a.reshape(batch, l_pad, -1)
    y_rw, s_p = _scan(b3(r), b3(dec), b3(kx), b3(vx), b3(av), b3(bv),
                      jnp.zeros((batch * RWKV_HEADS, HEAD_DIM, HEAD_DIM), F32), nb=batch, tc=64)
    y_at = _attn_prompt(sinks, b3(q), b3(ka), b3(va), front=front)
    x1, h2, scores = _mix(xp, y_rw, b3(gg), b3(bon), y_at, mix_params, tm=ATT_BLOCK, skip=head_rows // ATT_BLOCK)
    y_prompt = ffn(x1, h2, scores).reshape(batch, seq, D_MODEL)
    keep = min(WINDOW, l_real)
    k_win_p = b3(ka)[:, l_pad - keep:].reshape(1, batch, keep, KV_HEADS, HEAD_DIM)
    v_win_p = b3(va)[:, l_pad - keep:].reshape(1, batch, keep, KV_HEADS, HEAD_DIM)
    wkv_p = s_p.reshape(1, batch, RWKV_HEADS, HEAD_DIM, HEAD_DIM)

    n_s = dec_batch * dec_seq
    xs = x_sample.reshape(n_s, D_MODEL)
    proj_s = _inproj(xs, g1, win, norm=True, tm=256)
    prev_proj = _inproj(state_shift[0], g1, win, norm=False, tm=dec_batch)
    prev_rows = jnp.pad(prev_proj[:, None, :], ((0, 0), (0, dec_seq - 1), (0, 0))).reshape(n_s, N_IN_COLS)
    pos_s = jnp.tile(PAST_LEN + jnp.arange(dec_seq), dec_batch)
    r, dec, kx, vx, av, bv, gg, bon, q, ka, va = _prep(proj_s, prev_rows, _rope_tables(pos_s), prep_params,
                                                       seq_rows=dec_seq, tm=n_s, sample=True)
    s3 = lambda a: a.reshape(dec_batch, dec_seq, -1)
    y_rw, s_s = _scan(s3(r), s3(dec), s3(kx), s3(vx), s3(av), s3(bv),
                      state_wkv[0].reshape(dec_batch * RWKV_HEADS, HEAD_DIM, HEAD_DIM), nb=4, tc=dec_seq)
    k_buf = cache_k_win[0].reshape(dec_batch, WINDOW, KV_WIDTH)
    v_buf = cache_v_win[0].reshape(dec_batch, WINDOW, KV_WIDTH)
    pad8 = lambda a: jnp.pad(a, ((0, 0), (0, SUBLANES - dec_seq), (0, 0)))
    y_at = _attn_sample(sinks, s3(q), pad8(s3(ka)), pad8(s3(va)), k_buf, v_buf, bb=8)
    one = lambda a: a.reshape(1, n_s, -1)
    x1, h2, scores = _mix(one(xs), one(y_rw), one(gg), one(bon), one(y_at), mix_params, tm=ATT_BLOCK, skip=0)
    y_sample = ffn(x1, h2, scores).reshape(dec_batch, dec_seq, D_MODEL)
    k_win_s = jnp.concatenate([k_buf, s3(ka)], axis=1)[:, -WINDOW:].reshape(1, dec_batch, WINDOW, KV_HEADS, HEAD_DIM)
    v_win_s = jnp.concatenate([v_buf, s3(va)], axis=1)[:, -WINDOW:].reshape(1, dec_batch, WINDOW, KV_HEADS, HEAD_DIM)
    wkv_s = s_s.reshape(1, dec_batch, RWKV_HEADS, HEAD_DIM, HEAD_DIM)

    last = jnp.concatenate([x_prompt[:, -1], x_sample[:, -1]], axis=0)
    pad_rows = (-last.shape[0]) % SUBLANES
    shift = _rmsnorm_rows(jnp.pad(last, ((0, pad_rows), (0, 0))), g1)
    shift_p = shift[:batch].reshape(1, batch, D_MODEL)
    shift_s = shift[batch:batch + dec_batch].reshape(1, dec_batch, D_MODEL)

    return (y_prompt, y_sample, k_win_p, v_win_p, wkv_p, shift_p, k_win_s, v_win_s, wkv_s, shift_s)
```
